```python
import math
import jax
import jax.numpy as jnp
from jax import lax
import numpy as np

D_MODEL = 1024
BATCH = 4
SEQ = 4096
DEPTH = 2

MEM_LEN = 256
EPS = 1e-6

MLA_HEADS = 4
MLA_Q_RANK = 256
MLA_KV_RANK = 128
MLA_NOPE = 128
MLA_ROPE = 64
MLA_V = 128
MLA_QK = MLA_NOPE + MLA_ROPE
ROPE_THETA = 10000.0
Q_BLOCK = 128

HG_HEADS = 4
HG_DK = 128
HG_DV = 128
HG_KW = HG_HEADS * HG_DK
HG_VW = HG_HEADS * HG_DV
HG_CHUNK = 64

S5_WIDTH = 512
S5_GROUP = 16
S5_GROUPS = S5_WIDTH // S5_GROUP
S5_STATE = 64

RW_HEADS = 8
RW_HEAD = 64
RW_WIDTH = RW_HEADS * RW_HEAD
RW_DECAY_LORA = 64
RW_AAA_LORA = 64
RW_GATE_LORA = 128
RW_MV_LORA = 32
RW_GN_EPS = 64e-5
RW_COLS = 3 * RW_WIDTH + RW_DECAY_LORA + RW_AAA_LORA + RW_GATE_LORA

N_BRANCH = 4
BRANCH_WIDTH = 512

X_HEADS = 4
X_HEAD_DIM = D_MODEL // X_HEADS

D_FF = 2816
CONV_W = 3

COL_SIZES = (MLA_Q_RANK, MLA_KV_RANK, MLA_ROPE,
             HG_KW, HG_KW, HG_VW, HG_VW,
             S5_WIDTH,
             RW_COLS,
             N_BRANCH * D_MODEL)
P_IN = MLA_Q_RANK + MLA_KV_RANK + MLA_ROPE + 2 * HG_KW + 2 * HG_VW + S5_WIDTH + RW_COLS + N_BRANCH * D_MODEL
RW_SIZES = (RW_WIDTH, RW_WIDTH, RW_WIDTH, RW_DECAY_LORA, RW_AAA_LORA, RW_GATE_LORA)

kernel_name = "hybrid_gated_mla_hgrn2_s5_rwkv7_block"


def split_cols(t, sizes):
    out, start = [], 0
    for n in sizes:
        out.append(t[..., start:start + n])
        start += n
    return out


def rmsnorm(x, g, eps=EPS):
    xf = x.astype(jnp.float32)
    y = xf * lax.rsqrt(jnp.mean(xf * xf, axis=-1, keepdims=True) + eps)
    return (y * g.astype(jnp.float32)).astype(x.dtype)


def rope_tables(positions):
    half = MLA_ROPE // 2
    inv_freq = ROPE_THETA ** (-jnp.arange(half, dtype=jnp.float32) / half)
    ang = positions.astype(jnp.float32)[..., None] * inv_freq
    return jnp.cos(ang), jnp.sin(ang)


def apply_rope(t, cos, sin):
    half = t.shape[-1] // 2
    t1, t2 = t[..., :half], t[..., half:]
    cos = cos.astype(t.dtype)
    sin = sin.astype(t.dtype)
    return jnp.concatenate([t1 * cos - t2 * sin, t1 * sin + t2 * cos], axis=-1)


def mla(cq, ckv, k_rope, cos, sin, q_norm, w_uq, kv_norm, w_ukv):
    B, S, _ = cq.shape
    H = MLA_HEADS
    q = (rmsnorm(cq, q_norm) @ w_uq).reshape(B, S, H, MLA_QK)
    kv = (rmsnorm(ckv, kv_norm) @ w_ukv).reshape(B, S, H, MLA_NOPE + MLA_V)
    q = jnp.concatenate([q[..., :MLA_NOPE],
                         apply_rope(q[..., MLA_NOPE:], cos[:, :, None], sin[:, :, None])], axis=-1)
    k_pe = apply_rope(k_rope, cos, sin)
    k = jnp.concatenate([kv[..., :MLA_NOPE],
                         jnp.broadcast_to(k_pe[:, :, None, :], (B, S, H, MLA_ROPE))], axis=-1)
    v = kv[..., MLA_NOPE:]
    q, k, v = (t.transpose(0, 2, 1, 3) for t in (q, k, v))
    nb = S // Q_BLOCK
    q_blocks = q.reshape(B, H, nb, Q_BLOCK, MLA_QK).transpose(2, 0, 1, 3, 4)
    k_pos = jnp.arange(S)
    scale = MLA_QK ** -0.5

    def one_block(args):
        qb, i = args
        s = jnp.einsum('bhqd,bhkd->bhqk', qb, k).astype(jnp.float32) * scale
        q_pos = i * Q_BLOCK + jnp.arange(Q_BLOCK)
        s = jnp.where(k_pos[None, :] <= q_pos[:, None], s, -jnp.inf)
        p = jax.nn.softmax(s, axis=-1).astype(v.dtype)
        return jnp.einsum('bhqk,bhkd->bhqd', p, v)

    o = lax.map(one_block, (q_blocks, jnp.arange(nb)))
    return o.transpose(1, 0, 3, 2, 4).reshape(B, S, H * MLA_V)


def hgrn2(qc, fc, ic, gc, lb, o_norm):
    B, S, _ = qc.shape
    H, C = HG_HEADS, HG_CHUNK
    f = lb + (1.0 - lb) * jax.nn.sigmoid(fc.astype(jnp.float32))
    logf = jnp.log(f)
    k = 1.0 - f

    def chunks(t, d):
        return t.astype(jnp.float32).reshape(B, S // C, C, H, d).transpose(1, 0, 3, 2, 4)

    xs = (chunks(qc, HG_DK), chunks(k, HG_DK), chunks(ic, HG_DV), chunks(logf, HG_DK))
    causal = jnp.tril(jnp.ones((C, C), dtype=bool))

    def step(state, inp):
        q, kk, v, lf = inp
        b = jnp.cumsum(lf, axis=2)
        diff = jnp.where(causal[:, :, None], b[:, :, :, None, :] - b[:, :, None, :, :], -jnp.inf)
        att = jnp.einsum('bhtk,bhsk,bhtsk->bhts', q, kk, jnp.exp(diff))
        o = (jnp.einsum('bhts,bhsv->bhtv', att, v)
             + jnp.einsum('bhtk,bhkv->bhtv', q * jnp.exp(b), state))
        b_last = b[:, :, -1:, :]
        state = (state * jnp.exp(b_last)[:, :, 0, :, None]
                 + jnp.einsum('bhsk,bhsv->bhkv', kk * jnp.exp(b_last - b), v))
        return state, o

    s0 = jnp.zeros((B, H, HG_DK, HG_DV), jnp.float32)
    _, o = lax.scan(step, s0, xs)
    o = o.transpose(1, 0, 3, 2, 4).reshape(B, S, H, HG_DV)
    o = o * lax.rsqrt(jnp.mean(o * o, axis=-1, keepdims=True) + EPS) * o_norm.astype(jnp.float32).reshape(H, HG_DV)
    o = o.reshape(B, S, HG_VW) * jax.nn.sigmoid(gc.astype(jnp.float32))
    return o.astype(qc.dtype)


def s5(u, A_re, A_im, log_step, B_re, B_im, C_re, C_im, D, w_glu, b_glu):
    Bn, S, _ = u.shape
    G, N = S5_GROUPS, S5_STATE
    f32 = jnp.float32
    uf = u.astype(f32).reshape(Bn, S, G, S5_GROUP)
    a_re = jnp.minimum(A_re.astype(f32), -1e-4)
    a_im = A_im.astype(f32)
    dt = jnp.exp(log_step.astype(f32))[:, None]
    mag = jnp.exp(dt * a_re)
    ab_re = mag * jnp.cos(dt * a_im)
    ab_im = mag * jnp.sin(dt * a_im)
    den = a_re * a_re + a_im * a_im
    z_re = ((ab_re - 1.0) * a_re + ab_im * a_im) / den
    z_im = (ab_im * a_re - (ab_re - 1.0) * a_im) / den
    Br, Bi = B_re.astype(f32), B_im.astype(f32)
    bb_re = z_re[..., None] * Br - z_im[..., None] * Bi
    bb_im = z_re[..., None] * Bi + z_im[..., None] * Br
    bu_re = jnp.einsum('bsgc,gnc->bsgn', uf, bb_re)
    bu_im = jnp.einsum('bsgc,gnc->bsgn', uf, bb_im)
    a_re_s = jnp.broadcast_to(ab_re[None, None], (Bn, S, G, N))
    a_im_s = jnp.broadcast_to(ab_im[None, None], (Bn, S, G, N))

    def combine(e1, e2):
        a1r, a1i, b1r, b1i = e1
        a2r, a2i, b2r, b2i = e2
        return (a2r * a1r - a2i * a1i,
                a2r * a1i + a2i * a1r,
                a2r * b1r - a2i * b1i + b2r,
                a2r * b1i + a2i * b1r + b2i)

    _, _, x_re, x_im = lax.associative_scan(combine, (a_re_s, a_im_s, bu_re, bu_im), axis=1)
    y = (jnp.einsum('bsgn,gcn->bsgc', x_re, C_re.astype(f32))
         - jnp.einsum('bsgn,gcn->bsgc', x_im, C_im.astype(f32)))
    y = y.reshape(Bn, S, S5_WIDTH) + D.astype(f32) * u.astype(f32)
    z = jax.nn.gelu(y)
    out = z * jax.nn.sigmoid(z @ w_glu.astype(f32) + b_glu.astype(f32))
    return out.astype(u.dtype)


def token_shift_mix(t, mu):
    prev = jnp.pad(t, ((0, 0), (1, 0), (0, 0)))[:, :-1]
    return t + (prev - t) * mu


def rwkv7(m, h, v_first, vres, w0, w_up, a0, a_up, g_up, k_k, k_a, r_k, ln_w, ln_b):
    B, S, _ = m.shape
    f32 = jnp.float32
    r, k, v, wd, ad, gd = split_cols(m, RW_SIZES)
    w_log = -jax.nn.softplus(-(w0 + jnp.tanh(wd) @ w_up)) - 0.5
    decay = jnp.exp(-jnp.exp(w_log.astype(f32)))
    a = jax.nn.sigmoid(a0 + ad @ a_up)
    g = jax.nn.sigmoid(gd) @ g_up
    if vres is not None:
        v_down, v_up, v_bias = vres
        v = v + (v_first - v) * jax.nn.sigmoid(v_bias + (h @ v_down) @ v_up)
    v_out = v

    def heads(t):
        return t.astype(f32).reshape(B, S, RW_HEADS, RW_HEAD)

    kk = heads(k * k_k)
    kk = kk * lax.rsqrt(jnp.sum(kk * kk, axis=-1, keepdims=True) + 1e-12)
    a_h = heads(a)
    k_a_h = k_a.astype(f32).reshape(RW_HEADS, RW_HEAD)
    k_h = heads(k) * (1.0 + (a_h - 1.0) * k_a_h)
    r_h, v_h, w_h = heads(r), heads(v), heads(decay)

    def step(st, inp):
        r_t, k_t, v_t, w_t, kk_t, a_t = inp
        sa = jnp.einsum('bhvk,bhk->bhv', st, -kk_t)
        st = (st * w_t[:, :, None, :] + sa[..., None] * (kk_t * a_t)[:, :, None, :]
              + v_t[..., None] * k_t[:, :, None, :])
        return st, jnp.einsum('bhvk,bhk->bhv', st, r_t)

    xs = tuple(t.transpose(1, 0, 2, 3) for t in (r_h, k_h, v_h, w_h, kk, a_h))
    s0 = jnp.zeros((B, RW_HEADS, RW_HEAD, RW_HEAD), f32)
    _, y = lax.scan(step, s0, xs)
    y = y.transpose(1, 0, 2, 3)
    mu = jnp.mean(y, axis=-1, keepdims=True)
    var = jnp.mean((y - mu) ** 2, axis=-1, keepdims=True)
    y = ((y - mu) * lax.rsqrt(var + RW_GN_EPS)).reshape(B, S, RW_WIDTH) * ln_w.astype(f32) + ln_b.astype(f32)
    bonus = jnp.sum(r_h * k_h * r_k.astype(f32), axis=-1, keepdims=True) * v_h
    y = (y + bonus.reshape(B, S, RW_WIDTH)) * g.astype(f32)
    return y.astype(m.dtype), v_out


def cross_attn(hq, hm, w_q, w_kv, w_o):
    B, S, D = hq.shape
    M = hm.shape[1]
    q = (hq @ w_q).reshape(B, S, X_HEADS, X_HEAD_DIM)
    kv = (hm @ w_kv).reshape(B, M, 2, X_HEADS, X_HEAD_DIM)
    k, v = kv[:, :, 0], kv[:, :, 1]
    s = jnp.einsum('bshd,bmhd->bhsm', q, k).astype(jnp.float32) * (X_HEAD_DIM ** -0.5)
    p = jax.nn.softmax(s, axis=-1).astype(v.dtype)
    o = jnp.einsum('bhsm,bmhd->bshd', p, v).reshape(B, S, D)
    return o @ w_o


def conv_ffn(h, w_gate_up, conv_w, conv_b, w_down):
    gu = h @ w_gate_up
    gate, up = gu[..., :D_FF], gu[..., D_FF:]
    gate = lax.conv_general_dilated(gate, conv_w[:, None, :].astype(gate.dtype), window_strides=(1,),
                                    padding=[(CONV_W - 1, 0)], dimension_numbers=('NWC', 'WIO', 'NWC'),
                                    feature_group_count=D_FF) + conv_b
    return (jax.nn.silu(gate) * up) @ w_down


def setup_inputs(seed: int = 0) -> dict:
    key = jax.random.key(seed)
    ks = list(jax.random.split(key, 64))

    def nk():
        return ks.pop()

    def nrm(shape, scale):
        return scale * jax.random.normal(nk(), shape, jnp.float32)

    def gain(shape):
        return 1.0 + nrm(shape, 0.02)

    L, D = DEPTH, D_MODEL
    x = nrm((BATCH, SEQ, D), 1.0)
    mem = nrm((BATCH, MEM_LEN, D), 1.0)
    offsets = jax.random.randint(nk(), (BATCH, 1), 0, 1024, dtype=jnp.int32)
    positions = offsets + jnp.arange(SEQ, dtype=jnp.int32)[None, :]
    n_idx = jnp.arange(S5_STATE, dtype=jnp.float32)
    return {
        "x": x,
        "mem": mem,
        "positions": positions,
        "norm_mix": gain((L, D)),
        "w_in": nrm((L, D, P_IN), D ** -0.5),
        "mla_q_norm": gain((L, MLA_Q_RANK)),
        "mla_w_uq": nrm((L, MLA_Q_RANK, MLA_HEADS * MLA_QK), MLA_Q_RANK ** -0.5),
        "mla_kv_norm": gain((L, MLA_KV_RANK)),
        "mla_w_ukv": nrm((L, MLA_KV_RANK, MLA_HEADS * (MLA_NOPE + MLA_V)), MLA_KV_RANK ** -0.5),
        "hgrn_lb_logits": nrm((L, HG_KW), 0.1),
        "hgrn_o_norm": gain((L, HG_VW)),
        "s5_A_re": -0.5 + nrm((L, S5_GROUPS, S5_STATE), 0.01),
        "s5_A_im": math.pi * n_idx + nrm((L, S5_GROUPS, S5_STATE), 0.01),
        "s5_log_step": jax.random.uniform(nk(), (L, S5_GROUPS), jnp.float32, math.log(1e-3), math.log(1e-1)),
        "s5_B_re": nrm((L, S5_GROUPS, S5_STATE, S5_GROUP), S5_GROUP ** -0.5),
        "s5_B_im": nrm((L, S5_GROUPS, S5_STATE, S5_GROUP), S5_GROUP ** -0.5),
        "s5_C_re": nrm((L, S5_GROUPS, S5_GROUP, S5_STATE), 0.25),
        "s5_C_im": nrm((L, S5_GROUPS, S5_GROUP, S5_STATE), 0.25),
        "s5_D": nrm((L, S5_WIDTH), 1.0),
        "s5_w_glu": nrm((L, S5_WIDTH, S5_WIDTH), S5_WIDTH ** -0.5),
        "s5_b_glu": nrm((L, S5_WIDTH), 0.01),
        "rwkv_mu": jax.random.uniform(nk(), (L, RW_COLS), jnp.float32),
        "rwkv_w0": jax.random.uniform(nk(), (L, RW_WIDTH), jnp.float32, -5.0, 1.0),
        "rwkv_w_up": nrm((L, RW_DECAY_LORA, RW_WIDTH), 0.5 * RW_DECAY_LORA ** -0.5),
        "rwkv_a0": nrm((L, RW_WIDTH), 0.1),
        "rwkv_a_up": nrm((L, RW_AAA_LORA, RW_WIDTH), 0.5 * RW_AAA_LORA ** -0.5),
        "rwkv_g_up": nrm((L, RW_GATE_LORA, RW_WIDTH), RW_GATE_LORA ** -0.5),
        "rwkv_k_k": 0.85 + nrm((L, RW_WIDTH), 0.02),
        "rwkv_k_a": gain((L, RW_WIDTH)),
        "rwkv_r_k": nrm((L, RW_HEADS, RW_HEAD), 0.1),
        "rwkv_ln_w": gain((L, RW_WIDTH)),
        "rwkv_ln_b": nrm((L, RW_WIDTH), 0.01),
        "rwkv_vres_down": nrm((L - 1, D, RW_MV_LORA), D ** -0.5),
        "rwkv_vres_up": nrm((L - 1, RW_MV_LORA, RW_WIDTH), RW_MV_LORA ** -0.5),
        "rwkv_vres_bias": nrm((L - 1, RW_WIDTH), 0.1),
        "w_branch_mla": nrm((L, BRANCH_WIDTH, D), BRANCH_WIDTH ** -0.5),
        "w_branch_hgrn": nrm((L, BRANCH_WIDTH, D), BRANCH_WIDTH ** -0.5),
        "w_branch_s5": nrm((L, BRANCH_WIDTH, D), BRANCH_WIDTH ** -0.5),
        "w_branch_rwkv": nrm((L, BRANCH_WIDTH, D), BRANCH_WIDTH ** -0.5),
        "w_out": nrm((L, D, D), D ** -0.5),
        "norm_xq": gain((L, D)),
        "norm_xm": gain((L, D)),
        "xattn_w_q": nrm((L, D, D), D ** -0.5),
        "xattn_w_kv": nrm((L, D, 2 * D), D ** -0.5),
        "xattn_w_o": nrm((L, D, D), D ** -0.5),
        "norm_ffn": gain((L, D)),
        "ffn_w_gate_up": nrm((L, D, 2 * D_FF), D ** -0.5),
        "ffn_conv_w": nrm((L, CONV_W, D_FF), CONV_W ** -0.5),
        "ffn_conv_b": nrm((L, D_FF), 0.01),
        "ffn_w_down": nrm((L, D_FF, D), D_FF ** -0.5),
        "norm_final": gain((D,)),
    }


def reference(x, mem, positions, norm_mix, w_in, mla_q_norm, mla_w_uq, mla_kv_norm, mla_w_ukv,
              hgrn_lb_logits, hgrn_o_norm,
              s5_A_re, s5_A_im, s5_log_step, s5_B_re, s5_B_im, s5_C_re, s5_C_im, s5_D, s5_w_glu, s5_b_glu,
              rwkv_mu, rwkv_w0, rwkv_w_up, rwkv_a0, rwkv_a_up, rwkv_g_up, rwkv_k_k, rwkv_k_a, rwkv_r_k,
              rwkv_ln_w, rwkv_ln_b, rwkv_vres_down, rwkv_vres_up, rwkv_vres_bias,
              w_branch_mla, w_branch_hgrn, w_branch_s5, w_branch_rwkv, w_out,
              norm_xq, norm_xm, xattn_w_q, xattn_w_kv, xattn_w_o,
              norm_ffn, ffn_w_gate_up, ffn_conv_w, ffn_conv_b, ffn_w_down, norm_final):
    B, S, D = x.shape
    lb_p = jax.nn.softmax(hgrn_lb_logits.astype(jnp.float32), axis=0)
    lb_c = jnp.cumsum(lb_p, axis=0)
    lower_bounds = lb_c - lb_c[0:1]
    cos, sin = rope_tables(positions)
    v_first = None
    for l in range(DEPTH):
        h = rmsnorm(x, norm_mix[l])
        p = h @ w_in[l]
        cq, ckv, krope, hq, hf, hi, hg, su, rw, gate_logits = split_cols(p, COL_SIZES)
        o_mla = mla(cq, ckv, krope, cos, sin, mla_q_norm[l], mla_w_uq[l], mla_kv_norm[l], mla_w_ukv[l])
        o_hg = hgrn2(hq, hf, hi, hg, lower_bounds[l], hgrn_o_norm[l])
        o_s5 = s5(su, s5_A_re[l], s5_A_im[l], s5_log_step[l], s5_B_re[l], s5_B_im[l],
                  s5_C_re[l], s5_C_im[l], s5_D[l], s5_w_glu[l], s5_b_glu[l])
        vres = None if l == 0 else (rwkv_vres_down[l - 1], rwkv_vres_up[l - 1], rwkv_vres_bias[l - 1])
        o_rw, v_l = rwkv7(token_shift_mix(rw, rwkv_mu[l]), h, v_first, vres,
                          rwkv_w0[l], rwkv_w_up[l], rwkv_a0[l], rwkv_a_up[l], rwkv_g_up[l],
                          rwkv_k_k[l], rwkv_k_a[l], rwkv_r_k[l], rwkv_ln_w[l], rwkv_ln_b[l])
        if l == 0:
            v_first = v_l
        gates = jax.nn.sigmoid(gate_logits).reshape(B, S, N_BRANCH, D)
        y = (gates[:, :, 0] * (o_mla @ w_branch_mla[l])
             + gates[:, :, 1] * (o_hg @ w_branch_hgrn[l])
             + gates[:, :, 2] * (o_s5 @ w_branch_s5[l])
             + gates[:, :, 3] * (o_rw @ w_branch_rwkv[l]))
        x = x + y @ w_out[l]
        x = x + cross_attn(rmsnorm(x, norm_xq[l]), rmsnorm(mem, norm_xm[l]),
                           xattn_w_q[l], xattn_w_kv[l], xattn_w_o[l])
        x = x + conv_ffn(rmsnorm(x, norm_ffn[l]), ffn_w_gate_up[l], ffn_conv_w[l], ffn_conv_b[l], ffn_w_down[l])
    return rmsnorm(x, norm_final)
```

```python
import functools
import math

import numpy as np
import jax
import jax.numpy as jnp
from jax import lax
from jax.experimental import pallas as pl
from jax.experimental.pallas import tpu as pltpu

F32 = jnp.float32
BF16 = jnp.bfloat16

V7X_VMEM_LIMIT_BYTES = 56 * 1024 * 1024
LANES = 128

EPS = 1e-6
MLA_HEADS = 4
MLA_Q_RANK = 256
MLA_KV_RANK = 128
MLA_NOPE = 128
MLA_ROPE = 64
MLA_V = 128
MLA_QK = MLA_NOPE + MLA_ROPE
ROPE_THETA = 10000.0
HG_HEADS = 4
HG_DK = 128
HG_W = HG_HEADS * HG_DK
S5_WIDTH = 512
S5_GROUP = 16
S5_GROUPS = S5_WIDTH // S5_GROUP
S5_STATE = 64
S5_CHUNK = 16
RW_HEADS = 8
RW_HEAD = 64
RW_WIDTH = RW_HEADS * RW_HEAD
RW_GN_EPS = 64e-5
RW_MV_LORA = 32
X_HEADS = 4
D_FF = 2816

P_HG = 0
P_RW = 2048
P_MLA = 4096
P_S5 = 4608
P_WIDTH = 5120

HG_CHUNK = 128
RW_CHUNK = 64
RW_GROUP_HEADS = 4
RW_GROUP_W = RW_GROUP_HEADS * RW_HEAD


def _cparams(*sem):
    return pltpu.CompilerParams(dimension_semantics=sem, vmem_limit_bytes=V7X_VMEM_LIMIT_BYTES)


def _dot(a, b):
    return jnp.dot(a, b, preferred_element_type=F32)


def _dot_nt(a, b):
    return lax.dot_general(a, b, (((1,), (1,)), ((), ())), preferred_element_type=F32)


def _dot_tn(a, b):
    return lax.dot_general(a, b, (((0,), (0,)), ((), ())), preferred_element_type=F32)


def _bf(x):
    return x.astype(BF16)


def _split2(x):
    hi = x.astype(BF16)
    lo = (x - hi.astype(F32)).astype(BF16)
    return hi, lo


def _split3(x):
    hi = x.astype(BF16)
    r = x - hi.astype(F32)
    mid = r.astype(BF16)
    lo = (r - mid.astype(F32)).astype(BF16)
    return hi, mid, lo


def _dot_exact_lhs(m_bf16, x):
    x0, x1, x2 = _split3(x)
    return _dot(m_bf16, x0) + _dot(m_bf16, x1) + _dot(m_bf16, x2)


def _dot_exact_rhs(x, m_bf16):
    x0, x1 = _split2(x)
    return _dot(x0, m_bf16) + _dot(x1, m_bf16)


def _dot_f32(a, b):
    a0, a1 = _split2(a)
    b0, b1 = _split2(b)
    return _dot(a0, b0) + _dot(a0, b1) + _dot(a1, b0)


def _rms(x, g):
    return x * lax.rsqrt(jnp.mean(x * x, axis=-1, keepdims=True) + EPS) * g


def _sigmoid(x):
    return 1.0 / (1.0 + jnp.exp(-x))


def _full(shape):
    n = len(shape)
    return pl.BlockSpec(shape, lambda *_: (0,) * n)


def _norm_proj_kernel(x_ref, g_ref, w_ref, o_ref, hn_ref, *, act):
    @pl.when(pl.program_id(1) == 0)
    def _():
        hn_ref[...] = _bf(_rms(x_ref[...], g_ref[...]))

    y = _dot(hn_ref[...], w_ref[...])
    if act == "sigmoid":
        y = _sigmoid(y)
    o_ref[...] = y.astype(o_ref.dtype)


def _norm_proj(x, g, w, *, act=None, out_dtype=F32, tm=512, tn=1024):
    M, K = x.shape
    N = w.shape[1]
    tm = min(tm, M)
    tn = min(tn, N)
    assert M % tm == 0 and N % tn == 0
    return pl.pallas_call(
        functools.partial(_norm_proj_kernel, act=act),
        grid=(M // tm, N // tn),
        in_specs=[pl.BlockSpec((tm, K), lambda i, j: (i, 0)),
                  pl.BlockSpec((1, K), lambda i, j: (0, 0)),
                  pl.BlockSpec((K, tn), lambda i, j: (0, j))],
        out_specs=pl.BlockSpec((tm, tn), lambda i, j: (i, j)),
        out_shape=jax.ShapeDtypeStruct((M, N), out_dtype),
        scratch_shapes=[pltpu.VMEM((tm, K), BF16)],
        compiler_params=_cparams("parallel", "arbitrary"),
    )(x, g.reshape(1, K), w)


def _mla_prep_kernel(p_ref, pos_ref, frq_ref, sgn_ref, qn_ref, wq_ref, kvn_ref, wkv_ref,
                     q_ref, k_ref, v_ref):
    p = p_ref[0]
    cq = p[:, 0:MLA_Q_RANK]
    ckv = p[:, MLA_Q_RANK:MLA_Q_RANK + MLA_KV_RANK]
    kr = p[:, 384:448]
    kr_sw = p[:, 448:512]
    ang = pos_ref[0] * frq_ref[...]
    cc = jnp.cos(ang)
    ss = jnp.sin(ang) * sgn_ref[...]
    qa = _dot(_bf(_rms(cq, qn_ref[...])), wq_ref[...])
    kva = _dot(_bf(_rms(ckv, kvn_ref[...])), wkv_ref[...])
    kpe = _bf(kr * cc + kr_sw * ss)
    scale = MLA_QK ** -0.5
    for h in range(MLA_HEADS):
        qb = h * 384
        q_pe = qa[:, qb + 128:qb + 192] * cc + qa[:, qb + 256:qb + 320] * ss
        q_ref[0, h, :, 0:MLA_NOPE] = _bf(qa[:, qb:qb + 128] * scale)
        q_ref[0, h, :, MLA_NOPE:MLA_QK] = _bf(q_pe * scale)
        kb = h * 256
        k_ref[0, h, :, 0:MLA_NOPE] = _bf(kva[:, kb:kb + 128])
        k_ref[0, h, :, MLA_NOPE:MLA_QK] = kpe
        v_ref[0, h] = _bf(kva[:, kb + 128:kb + 256])


def _mla_prep(p3, pos3, frq, sgn, qn, wq, kvn, wkv, *, tm=512):
    B, S, _ = p3.shape
    tm = min(tm, S)
    H = MLA_HEADS
    return pl.pallas_call(
        _mla_prep_kernel,
        grid=(B, S // tm),
        in_specs=[pl.BlockSpec((1, tm, 512), lambda b, i: (b, i, P_MLA // 512)),
                  pl.BlockSpec((1, tm, 1), lambda b, i: (b, i, 0)),
                  _full((1, MLA_ROPE)), _full((1, MLA_ROPE)),
                  _full((1, MLA_Q_RANK)), _full(wq.shape),
                  _full((1, MLA_KV_RANK)), _full(wkv.shape)],
        out_specs=[pl.BlockSpec((1, H, tm, MLA_QK), lambda b, i: (b, 0, i, 0)),
                   pl.BlockSpec((1, H, tm, MLA_QK), lambda b, i: (b, 0, i, 0)),
                   pl.BlockSpec((1, H, tm, MLA_V), lambda b, i: (b, 0, i, 0))],
        out_shape=[jax.ShapeDtypeStruct((B, H, S, MLA_QK), BF16),
                   jax.ShapeDtypeStruct((B, H, S, MLA_QK), BF16),
                   jax.ShapeDtypeStruct((B, H, S, MLA_V), BF16)],
        compiler_params=_cparams("parallel", "parallel"),
    )(p3, pos3, frq, sgn, qn, wq, kvn, wkv)


def _mla_attn_kernel(q_ref, k_ref, v_ref, o_ref, *, tq):
    i = pl.program_id(2)
    q = q_ref[0, 0]

    def step(k, v, carry, mask):
        m, l, acc = carry
        s = _dot_nt(q, k)
        if mask is not None:
            s = jnp.where(mask, s, -1e30)
        m_new = jnp.maximum(m, jnp.max(s, axis=-1, keepdims=True))
        alpha = jnp.exp(m - m_new)
        pr = jnp.exp(s - m_new)
        l = alpha * l + jnp.sum(pr, axis=-1, keepdims=True)
        acc = alpha * acc + _dot(_bf(pr), v)
        return m_new, l, acc

    def body(j, carry):
        off = pl.multiple_of(j * tq, tq)
        return step(k_ref[0, 0, pl.ds(off, tq), :], v_ref[0, 0, pl.ds(off, tq), :], carry, None)

    init = (jnp.full((tq, 1), -1e30, F32), jnp.zeros((tq, 1), F32), jnp.zeros((tq, MLA_V), F32))
    carry = lax.fori_loop(0, i, body, init)
    off = pl.multiple_of(i * tq, tq)
    row = lax.broadcasted_iota(jnp.int32, (tq, tq), 0)
    col = lax.broadcasted_iota(jnp.int32, (tq, tq), 1)
    _, l, acc = step(k_ref[0, 0, pl.ds(off, tq), :], v_ref[0, 0, pl.ds(off, tq), :], carry, col <= row)
    o_ref[0] = _bf(acc / l)


def _mla_attn(q, k, v, *, tq=512):
    B, H, S, _ = q.shape
    tq = min(tq, S)
    return pl.pallas_call(
        functools.partial(_mla_attn_kernel, tq=tq),
        grid=(B, H, S // tq),
        in_specs=[pl.BlockSpec((1, 1, tq, MLA_QK), lambda b, h, i: (b, h, i, 0)),
                  pl.BlockSpec((1, 1, S, MLA_QK), lambda b, h, i: (b, h, 0, 0)),
                  pl.BlockSpec((1, 1, S, MLA_V), lambda b, h, i: (b, h, 0, 0))],
        out_specs=pl.BlockSpec((1, tq, MLA_V), lambda b, h, i: (b, i, h)),
        out_shape=jax.ShapeDtypeStruct((B, S, H * MLA_V), BF16),
        compiler_params=_cparams("parallel", "parallel", "parallel"),
    )(q, k, v)


def _hgrn_tables(C):
    t = np.arange(C)
    nlev = int(math.log2(C))
    masks, sels, sgns = [], [], []
    for lev in range(nlev):
        n = C >> (lev + 1)
        blk2 = t // (2 * n)
        upper = (t // n) % 2
        masks.append((blk2[:, None] == blk2[None, :]) & (upper[:, None] == 1) & (upper[None, :] == 0))
        sel = np.zeros((C, C), np.float32)
        sel[t, blk2 * 2 * n + n - 1] = 1.0
        sels.append(sel)
        sgns.append(np.broadcast_to(np.where(upper == 1, 1.0, -1.0)[:, None], (C, HG_DK)))
    masks.append(np.eye(C, dtype=bool))
    tri = (t[:, None] >= t[None, :]).astype(np.float32)
    return (jnp.asarray(np.stack(masks), F32), jnp.asarray(np.concatenate(sels, 0), BF16),
            jnp.asarray(np.stack(sgns), F32), jnp.asarray(tri, BF16))


def _hgrn_kernel(p_ref, lb_ref, onorm_ref, mask_ref, sel_ref, sgn_ref, tri_ref, o_ref, st_ref, *, C):
    @pl.when(pl.program_id(1) == 0)
    def _():
        st_ref[...] = jnp.zeros_like(st_ref)

    nlev = mask_ref.shape[0] - 1
    x = p_ref[0]
    q = x[:, 0:HG_W]
    fc = x[:, HG_W:2 * HG_W]
    v = x[:, 2 * HG_W:3 * HG_W]
    g = x[:, 3 * HG_W:4 * HG_W]
    lb = lb_ref[...]
    f = lb + (1.0 - lb) * _sigmoid(fc)
    logf = jnp.log(f)
    kk = 1.0 - f
    b = _dot_exact_lhs(tri_ref[...], logf)
    b0, b1 = _split2(b)
    bmid = _dot(sel_ref[...], b0) + _dot(sel_ref[...], b1)
    b_last = b[C - 1:C, :]
    qdec = q * jnp.exp(b)
    kdec = kk * jnp.exp(b_last - b)
    for h in range(HG_HEADS):
        sl = slice(h * HG_DK, (h + 1) * HG_DK)
        qh, kh, vh, bh = q[:, sl], kk[:, sl], _bf(v[:, sl]), b[:, sl]
        att = _dot_nt(_bf(qh), _bf(kh)) * mask_ref[nlev]
        for lev in range(nlev):
            e = jnp.exp(sgn_ref[lev] * (bh - bmid[lev * C:(lev + 1) * C, sl]))
            att = att + _dot_nt(_bf(qh * e), _bf(kh * e)) * mask_ref[lev]
        st = st_ref[h]
        o = _dot(_bf(att), vh) + _dot_nt(_bf(qdec[:, sl]), _bf(st))
        st_ref[h] = st * jnp.exp(b_last[:, sl]) + _dot_tn(vh, _bf(kdec[:, sl]))
        o = o * lax.rsqrt(jnp.mean(o * o, axis=-1, keepdims=True) + EPS) * onorm_ref[:, sl]
        o_ref[0, :, sl] = _bf(o * _sigmoid(g[:, sl]))


def _hgrn(p3, lb, onorm, *, C=HG_CHUNK):
    B, S, _ = p3.shape
    C = min(C, S)
    masks, sel, sgn, tri = _hgrn_tables(C)
    return pl.pallas_call(
        functools.partial(_hgrn_kernel, C=C),
        grid=(B, S // C),
        in_specs=[pl.BlockSpec((1, C, 4 * HG_W), lambda b, i: (b, i, P_HG // (4 * HG_W))),
                  _full((1, HG_W)), _full((1, HG_W)),
                  _full(masks.shape), _full(sel.shape), _full(sgn.shape), _full(tri.shape)],
        out_specs=pl.BlockSpec((1, C, HG_W), lambda b, i: (b, i, 0)),
        out_shape=jax.ShapeDtypeStruct((B, S, HG_W), BF16),
        scratch_shapes=[pltpu.VMEM((HG_HEADS, HG_DK, HG_DK), F32)],
        compiler_params=_cparams("parallel", "arbitrary"),
    )(p3, lb.reshape(1, HG_W), onorm.reshape(1, HG_W), masks, sel, sgn, tri)


def _s5_tables(A_re, A_im, log_step, B_re, B_im, C_re, C_im, nsteps):
    L = S5_CHUNK
    G, N, W = S5_GROUPS, S5_STATE, S5_GROUP
    hp = lax.Precision.HIGHEST
    a_re = jnp.minimum(A_re.astype(F32), -1e-4)
    a_im = A_im.astype(F32)
    dt = jnp.exp(log_step.astype(F32))[:, None]
    lam_re, lam_im = dt * a_re, dt * a_im

    def apow(m):
        m = jnp.asarray(m, F32)[..., None, None]
        mag = jnp.exp(m * lam_re)
        return mag * jnp.cos(m * lam_im), mag * jnp.sin(m * lam_im)

    ab_re, ab_im = apow(1.0)
    den = a_re * a_re + a_im * a_im
    z_re = ((ab_re - 1.0) * a_re + ab_im * a_im) / den
    z_im = (ab_im * a_re - (ab_re - 1.0) * a_im) / den
    Br, Bi = B_re.astype(F32), B_im.astype(F32)
    bb_re = z_re[..., None] * Br - z_im[..., None] * Bi
    bb_im = z_re[..., None] * Bi + z_im[..., None] * Br
    Cr, Ci = C_re.astype(F32), C_im.astype(F32)

    lag_re, lag_im = apow(jnp.arange(L))
    cb_re = jnp.einsum('gcn,lgn->lgcn', Cr, lag_re) - jnp.einsum('gcn,lgn->lgcn', Ci, lag_im)
    cb_im = jnp.einsum('gcn,lgn->lgcn', Cr, lag_im) + jnp.einsum('gcn,lgn->lgcn', Ci, lag_re)
    kern = (jnp.einsum('lgcn,gnd->lgcd', cb_re, bb_re, precision=hp)
            - jnp.einsum('lgcn,gnd->lgcd', cb_im, bb_im, precision=hp))
    s_idx = np.arange(L)[:, None]
    t_idx = np.arange(L)[None, :]
    lag = t_idx - s_idx
    toe = kern[np.clip(lag, 0, L - 1)]
    toe = jnp.where((lag >= 0)[:, :, None, None, None], toe, 0.0)
    toe = toe.transpose(2, 0, 4, 1, 3).reshape(G, L * W, L * W)

    e_re, e_im = apow(L - 1 - jnp.arange(L))
    w_re = e_re[..., None] * bb_re - e_im[..., None] * bb_im
    w_im = e_re[..., None] * bb_im + e_im[..., None] * bb_re
    w_re = w_re.transpose(1, 0, 3, 2).reshape(G, L * W, N)
    w_im = w_im.transpose(1, 0, 3, 2).reshape(G, L * W, N)
    w2 = jnp.zeros((G // 2, 2, L * W, 4, N), F32)
    w_re_p = w_re.reshape(G // 2, 2, L * W, N)
    w_im_p = w_im.reshape(G // 2, 2, L * W, N)
    w2 = w2.at[:, 0, :, 0].set(w_re_p[:, 0]).at[:, 1, :, 1].set(w_re_p[:, 1])
    w2 = w2.at[:, 0, :, 2].set(w_im_p[:, 0]).at[:, 1, :, 3].set(w_im_p[:, 1])
    w2 = w2.reshape(G // 2, 2 * L * W, 4 * N)

    o_re, o_im = apow(1 + jnp.arange(L))
    v_re = jnp.einsum('gcn,tgn->gntc', Cr, o_re) - jnp.einsum('gcn,tgn->gntc', Ci, o_im)
    v_im = -(jnp.einsum('gcn,tgn->gntc', Cr, o_im) + jnp.einsum('gcn,tgn->gntc', Ci, o_re))
    v_re = v_re.reshape(G // 2, 2, N, L * W)
    v_im = v_im.reshape(G // 2, 2, N, L * W)
    v2 = jnp.zeros((G // 2, 2, 2, N, 2, L * W), F32)
    v2 = v2.at[:, 0, 0, :, 0].set(v_re[:, 0]).at[:, 0, 1, :, 1].set(v_re[:, 1])
    v2 = v2.at[:, 1, 0, :, 0].set(v_im[:, 0]).at[:, 1, 1, :, 1].set(v_im[:, 1])
    v2 = v2.reshape(G // 2, 2, 2 * N, 2 * L * W)

    p_re, p_im = apow(L * (2 ** jnp.arange(nsteps)))
    return (toe.astype(BF16), w2.astype(BF16), v2.astype(BF16),
            p_re.reshape(nsteps, G * N), p_im.reshape(nsteps, G * N))


def _s5_state_kernel(u_ref, w2_ref, pre_ref, pim_ref, xre_ref, xim_ref, are_ref, aim_ref, bre_ref, bim_ref,
                     *, K, pad, nsteps):
    N2 = 2 * S5_STATE
    UW = 2 * S5_CHUNK * S5_GROUP
    zeros = jnp.zeros((pad, S5_GROUPS * S5_STATE), F32)
    for r in (are_ref, aim_ref, bre_ref, bim_ref):
        r[0:pad, :] = zeros
    for pr in range(S5_GROUPS // 2):
        wc = _dot(u_ref[0, :, pr * UW:(pr + 1) * UW], w2_ref[pr])
        are_ref[pad:pad + K, pr * N2:(pr + 1) * N2] = wc[:, 0:N2]
        aim_ref[pad:pad + K, pr * N2:(pr + 1) * N2] = wc[:, N2:2 * N2]
    bufs = ((are_ref, aim_ref), (bre_ref, bim_ref))
    for j in range(nsteps):
        d = 1 << j
        (sre, sim), (dre, dim) = bufs[j % 2], bufs[(j + 1) % 2]
        pr_, pi_ = pre_ref[j:j + 1, :], pim_ref[j:j + 1, :]
        cre, cim = sre[pad:pad + K, :], sim[pad:pad + K, :]
        hre, him = sre[pad - d:pad - d + K, :], sim[pad - d:pad - d + K, :]
        dre[pad:pad + K, :] = cre + pr_ * hre - pi_ * him
        dim[pad:pad + K, :] = cim + pr_ * him + pi_ * hre
    fre, fim = bufs[nsteps % 2]
    xre_ref[0] = _bf(fre[pad - 1:pad - 1 + K, :])
    xim_ref[0] = _bf(fim[pad - 1:pad - 1 + K, :])


def _s5_out_kernel(u_ref, xre_ref, xim_ref, toe_ref, v2_ref, y_ref):
    N2 = 2 * S5_STATE
    UW = S5_CHUNK * S5_GROUP
    for pr in range(S5_GROUPS // 2):
        ys = (_dot(xre_ref[0, :, pr * N2:(pr + 1) * N2], v2_ref[pr, 0])
              + _dot(xim_ref[0, :, pr * N2:(pr + 1) * N2], v2_ref[pr, 1]))
        for gi in range(2):
            g = 2 * pr + gi
            y = _dot(u_ref[0, :, g * UW:(g + 1) * UW], toe_ref[g]) + ys[:, gi * UW:(gi + 1) * UW]
            y_ref[0, :, g * UW:(g + 1) * UW] = _bf(y)


def _s5_post_kernel(y_ref, u_ref, d_ref, w_ref, b_ref, o_ref):
    y = y_ref[...].astype(F32) + d_ref[...] * u_ref[...]
    z = 0.5 * y * (1.0 + jnp.tanh(math.sqrt(2.0 / math.pi) * (y + 0.044715 * (y * y * y))))
    o_ref[...] = _bf(z * _sigmoid(_dot(_bf(z), w_ref[...]) + b_ref[...]))


def _s5(p, B, S, tabs, D, w_glu, b_glu, *, tm=512):
    T = B * S
    L, G, W = S5_CHUNK, S5_GROUPS, S5_GROUP
    K = S // L
    toe, w2, v2, p_re, p_im = tabs
    nsteps = p_re.shape[0]
    pad = max(8, K // 2)
    GW = G * L * W
    GN = G * S5_STATE
    u = p[:, P_S5:P_S5 + S5_WIDTH]
    uf = _bf(u).reshape(B, K, L, G, W).transpose(0, 1, 3, 2, 4).reshape(B, K, GW)
    xre, xim = pl.pallas_call(
        functools.partial(_s5_state_kernel, K=K, pad=pad, nsteps=nsteps),
        grid=(B,),
        in_specs=[pl.BlockSpec((1, K, GW), lambda b: (b, 0, 0)),
                  _full(w2.shape), _full(p_re.shape), _full(p_im.shape)],
        out_specs=[pl.BlockSpec((1, K, GN), lambda b: (b, 0, 0))] * 2,
        out_shape=[jax.ShapeDtypeStruct((B, K, GN), BF16)] * 2,
        scratch_shapes=[pltpu.VMEM((pad + K, GN), F32)] * 4,
        compiler_params=_cparams("parallel"),
    )(uf, w2, p_re, p_im)
    yf = pl.pallas_call(
        _s5_out_kernel,
        grid=(B,),
        in_specs=[pl.BlockSpec((1, K, GW), lambda b: (b, 0, 0)),
                  pl.BlockSpec((1, K, GN), lambda b: (b, 0, 0)),
                  pl.BlockSpec((1, K, GN), lambda b: (b, 0, 0)),
                  _full(toe.shape), _full(v2.shape)],
        out_specs=pl.BlockSpec((1, K, GW), lambda b: (b, 0, 0)),
        out_shape=jax.ShapeDtypeStruct((B, K, GW), BF16),
        compiler_params=_cparams("parallel"),
    )(uf, xre, xim, toe, v2)
    y = yf.reshape(B, K, G, L, W).transpose(0, 1, 3, 2, 4).reshape(T, S5_WIDTH)
    tm = min(tm, T)
    return pl.pallas_call(
        _s5_post_kernel,
        grid=(T // tm,),
        in_specs=[pl.BlockSpec((tm, S5_WIDTH), lambda i: (i, 0)),
                  pl.BlockSpec((tm, S5_WIDTH), lambda i: (i, P_S5 // S5_WIDTH)),
                  _full((1, S5_WIDTH)), _full((S5_WIDTH, S5_WIDTH)), _full((1, S5_WIDTH))],
        out_specs=pl.BlockSpec((tm, S5_WIDTH), lambda i: (i, 0)),
        out_shape=jax.ShapeDtypeStruct((T, S5_WIDTH), BF16),
        compiler_params=_cparams("parallel"),
    )(y, p, D.reshape(1, -1), _bf(w_glu), b_glu.reshape(1, -1))


def _rwkv_tables(C):
    GH, HD = RW_GROUP_HEADS, RW_HEAD
    i = np.arange(GH * C)
    same = (i[:, None] // C) == (i[None, :] // C)
    ti, si = i[:, None] % C, i[None, :] % C
    strict = (same & (si < ti)).astype(np.float32)
    incl = (same & (si <= ti)).astype(np.float32)
    eye = np.eye(GH * HD, dtype=np.float32)
    lane = np.arange(GH * HD)
    lanemask = np.zeros((8, GH * HD), np.float32)
    for h in range(GH):
        lanemask[h] = (lane // HD) == h
    c = np.arange(RW_WIDTH)
    ones = ((c[:, None] // HD) == (c[None, :] // HD)).astype(np.float32)
    t = np.arange(C)
    tri = (t[:, None] >= t[None, :]).astype(np.float32)
    return (jnp.asarray(strict), jnp.asarray(incl), jnp.asarray(eye), jnp.asarray(lanemask),
            jnp.asarray(ones, BF16), jnp.asarray(tri, BF16))


def _rwkv_kernel(*refs, C, has_vres):
    if has_vres:
        (p_ref, vf_ref, mu_ref, w0_ref, wup_ref, a0_ref, aup_ref, gup_ref, kk_ref, ka_ref, rk_ref,
         lnw_ref, lnb_ref, vup_ref, vb_ref,
         strict_ref, incl_ref, eye_ref, lm_ref, ones_ref, tri_ref, o_ref, st_ref, prev_ref) = refs
    else:
        (p_ref, mu_ref, w0_ref, wup_ref, a0_ref, aup_ref, gup_ref, kk_ref, ka_ref, rk_ref,
         lnw_ref, lnb_ref,
         strict_ref, incl_ref, eye_ref, lm_ref, ones_ref, tri_ref, o_ref, vout_ref, st_ref, prev_ref) = refs

    @pl.when(pl.program_id(1) == 0)
    def _():
        st_ref[...] = jnp.zeros_like(st_ref)
        prev_ref[...] = jnp.zeros_like(prev_ref)

    W = RW_WIDTH
    x = p_ref[0]
    rolled = pltpu.roll(x, 1, axis=0)
    row = lax.broadcasted_iota(jnp.int32, x.shape, 0)
    shifted = jnp.where(row == 0, prev_ref[0:1, :], rolled)
    prev_ref[0:1, :] = x[C - 1:C, :]
    m = x + (shifted - x) * mu_ref[...]
    r, k, v = m[:, 0:W], m[:, W:2 * W], m[:, 2 * W:3 * W]
    wd, ad, gd = m[:, 3 * W:3 * W + 128], m[:, 3 * W + 128:3 * W + 256], m[:, 3 * W + 256:3 * W + 384]
    zw = -(w0_ref[...] + _dot(_bf(jnp.tanh(wd)), wup_ref[...]))
    softplus = jnp.maximum(zw, 0.0) + jnp.log(1.0 + jnp.exp(-jnp.abs(zw)))
    logw = -jnp.exp(-softplus - 0.5)
    lr = _sigmoid(a0_ref[...] + _dot(_bf(ad), aup_ref[...]))
    gate = _dot(_bf(_sigmoid(gd)), gup_ref[...])
    if has_vres:
        vl = x[:, 3 * W + 384:3 * W + 512]
        v = v + (vf_ref[0] - v) * _sigmoid(vb_ref[...] + _dot(_bf(vl), vup_ref[...]))
    else:
        vout_ref[0] = v
    ones = ones_ref[...]
    kn = k * kk_ref[...]
    kn = kn * lax.rsqrt(_dot_exact_rhs(kn * kn, ones) + 1e-12)
    k2 = k * (1.0 + (lr - 1.0) * ka_ref[...])

    cum = _dot_exact_lhs(tri_ref[...], logw)
    p_in = jnp.exp(cum)
    p_inv = jnp.exp(-cum)
    p_end = jnp.exp(cum[C - 1:C, :] - cum)
    p_last = jnp.exp(cum[C - 1:C, :])
    a_t = -kn * jnp.exp(cum - logw)
    b_t = kn * lr * p_inv
    k_t = k2 * p_inv
    r_t = r * p_in
    b_e = kn * lr * p_end
    k_e = k2 * p_end

    strict, incl, eye = strict_ref[...], incl_ref[...], eye_ref[...]
    GW = RW_GROUP_W
    outs = []
    for gi in range(RW_HEADS // RW_GROUP_HEADS):
        sl = slice(gi * GW, (gi + 1) * GW)

        def stack(z):
            z = z[:, sl]
            return jnp.concatenate([z * lm_ref[h:h + 1, :] for h in range(RW_GROUP_HEADS)], axis=0)

        xa, xr, yb, yk, vs = stack(a_t), stack(r_t), _bf(stack(b_t)), _bf(stack(k_t)), _bf(stack(v))
        ybe, yke = _bf(stack(b_e)), _bf(stack(k_e))
        xa_b, xr_b = _bf(xa), _bf(xr)
        n = _dot_nt(xa_b, yb) * strict
        aak = _dot_nt(xa_b, yk) * strict
        arb = _bf(_dot_nt(xr_b, yb) * incl)
        ark = _bf(_dot_nt(xr_b, yk) * incl)
        tinv = eye + n
        pw = n
        for _ in range(int(math.log2(C)) - 1):
            pwb = _bf(pw)
            pw = _dot(pwb, pwb)
            tinv = tinv + _dot(_bf(pw), _bf(tinv))
        tb = _bf(tinv)
        a_hat = _dot(tb, xa_b)
        wmat = _dot(tb, _bf(_dot(_bf(aak), vs)))
        a_hat_b, w_b = _bf(a_hat), _bf(wmat)
        r_hat = xr + _dot(arb, a_hat_b)
        o_loc = _dot(arb, w_b) + _dot(ark, vs)
        st = st_ref[gi]
        o_st = _dot_nt(_bf(r_hat), _bf(st)) + o_loc
        o = o_st[0:C]
        for h in range(1, RW_GROUP_HEADS):
            o = o + o_st[h * C:(h + 1) * C]
        outs.append(o)
        mmat = eye * p_last[:, sl] + _dot_tn(a_hat_b, ybe)
        gmat = _dot_tn(w_b, ybe) + _dot_tn(vs, yke)
        st_ref[gi] = _dot_f32(st, mmat) + gmat
    y = jnp.concatenate(outs, axis=1)
    inv_n = 1.0 / RW_HEAD
    mean = _dot_exact_rhs(y, ones) * inv_n
    yc = y - mean
    var = _dot_exact_rhs(yc * yc, ones) * inv_n
    yn = yc * lax.rsqrt(var + RW_GN_EPS) * lnw_ref[...] + lnb_ref[...]
    bonus = _dot_exact_rhs(r * k2 * rk_ref[...], ones) * v
    o_ref[0] = _bf((yn + bonus) * gate)


def _rwkv(p3, v_first, prm, *, C=RW_CHUNK):
    B, S, _ = p3.shape
    C = min(C, S)
    has_vres = v_first is not None
    tabs = _rwkv_tables(C)
    W = RW_WIDTH
    NG = RW_HEADS // RW_GROUP_HEADS
    row = lambda a: a.reshape(1, -1)
    args = [p3]
    specs = [pl.BlockSpec((1, C, 2048), lambda b, i: (b, i, P_RW // 2048))]
    if has_vres:
        args.append(v_first)
        specs.append(pl.BlockSpec((1, C, W), lambda b, i: (b, i, 0)))
    small = [row(prm["mu"]), row(prm["w0"]), prm["w_up"], row(prm["a0"]), prm["a_up"], prm["g_up"],
             row(prm["k_k"]), row(prm["k_a"]), row(prm["r_k"]), row(prm["ln_w"]), row(prm["ln_b"])]
    if has_vres:
        small += [prm["v_up"], row(prm["v_bias"])]
    small += list(tabs)
    args += small
    specs += [_full(a.shape) for a in small]
    o_spec = pl.BlockSpec((1, C, W), lambda b, i: (b, i, 0))
    if has_vres:
        out_specs, out_shape = o_spec, jax.ShapeDtypeStruct((B, S, W), BF16)
    else:
        out_specs = [o_spec, o_spec]
        out_shape = [jax.ShapeDtypeStruct((B, S, W), BF16), jax.ShapeDtypeStruct((B, S, W), F32)]
    res = pl.pallas_call(
        functools.partial(_rwkv_kernel, C=C, has_vres=has_vres),
        grid=(B, S // C),
        in_specs=specs, out_specs=out_specs, out_shape=out_shape,
        scratch_shapes=[pltpu.VMEM((NG, RW_GROUP_W, RW_GROUP_W), F32), pltpu.VMEM((8, 2048), F32)],
        compiler_params=_cparams("parallel", "arbitrary"),
    )(*args)
    return (res, v_first) if has_vres else (res[0], res[1])


def _merge_kernel(x_ref, o0_ref, o1_ref, o2_ref, o3_ref, g_ref, wb_ref, wo_ref, y_ref):
    D = x_ref.shape[1]
    acc = None
    for m, o_ref in enumerate((o0_ref, o1_ref, o2_ref, o3_ref)):
        t = g_ref[:, m * D:(m + 1) * D].astype(F32) * _dot(o_ref[...], wb_ref[m])
        acc = t if acc is None else acc + t
    y_ref[...] = x_ref[...] + _dot(_bf(acc), wo_ref[...])


def _merge(x, outs, gates, wb, wo, *, tm=512):
    T, D = x.shape
    tm = min(tm, T)
    bw = outs[0].shape[1]
    return pl.pallas_call(
        _merge_kernel,
        grid=(T // tm,),
        in_specs=[pl.BlockSpec((tm, D), lambda i: (i, 0))]
        + [pl.BlockSpec((tm, bw), lambda i: (i, 0))] * 4
        + [pl.BlockSpec((tm, 4 * D), lambda i: (i, 0)), _full(wb.shape), _full(wo.shape)],
        out_specs=pl.BlockSpec((tm, D), lambda i: (i, 0)),
        out_shape=jax.ShapeDtypeStruct((T, D), F32),
        compiler_params=_cparams("parallel"),
    )(x, *outs, gates, wb, wo)


def _xattn_kernel(x_ref, kv_ref, g_ref, wq_ref, wo_ref, y_ref):
    x = x_ref[0]
    D = x.shape[1]
    hd = D // X_HEADS
    q = _bf(_dot(_bf(_rms(x, g_ref[...])), wq_ref[...]) * (hd ** -0.5))
    outs = []
    for h in range(X_HEADS):
        kh = kv_ref[0, :, h * hd:(h + 1) * hd]
        vh = kv_ref[0, :, D + h * hd:D + (h + 1) * hd]
        s = _dot_nt(q[:, h * hd:(h + 1) * hd], kh)
        e = jnp.exp(s - jnp.max(s, axis=-1, keepdims=True))
        pr = e / jnp.sum(e, axis=-1, keepdims=True)
        outs.append(_bf(_dot(_bf(pr), vh)))
    y_ref[0] = x + _dot(jnp.concatenate(outs, axis=1), wo_ref[...])


def _xattn(x3, kv3, g, wq, wo, *, tm=512):
    B, S, D = x3.shape
    M = kv3.shape[1]
    tm = min(tm, S)
    return pl.pallas_call(
        _xattn_kernel,
        grid=(B, S // tm),
        in_specs=[pl.BlockSpec((1, tm, D), lambda b, i: (b, i, 0)),
                  pl.BlockSpec((1, M, 2 * D), lambda b, i: (b, 0, 0)),
                  _full((1, D)), _full(wq.shape), _full(wo.shape)],
        out_specs=pl.BlockSpec((1, tm, D), lambda b, i: (b, i, 0)),
        out_shape=jax.ShapeDtypeStruct((B, S, D), F32),
        compiler_params=_cparams("parallel", "parallel"),
    )(x3, kv3, g.reshape(1, D), wq, wo)


HALO = 16


def _ffn_kernel(x_ref, halo_ref, g_ref, wg_ref, wu_ref, cw_ref, cb_ref, wd_ref, gf_ref, y_ref,
                hn_ref, gbuf_ref, acc_ref, *, tm, final_norm):
    i = pl.program_id(1)
    f = pl.program_id(2)

    @pl.when(f == 0)
    def _():
        hn_ref[0:HALO, :] = _bf(_rms(halo_ref[0], g_ref[...]))
        hn_ref[HALO:HALO + tm, :] = _bf(_rms(x_ref[0], g_ref[...]))
        acc_ref[...] = jnp.zeros_like(acc_ref)

    hn = hn_ref[...]
    gpre = _dot(hn, wg_ref[...])
    keep = (i > 0).astype(F32)
    gbuf_ref[0:HALO, :] = gpre[0:HALO] * keep
    gbuf_ref[HALO:HALO + tm, :] = gpre[HALO:HALO + tm]
    conv = (cw_ref[0:1, :] * gbuf_ref[HALO - 2:HALO - 2 + tm, :]
            + cw_ref[1:2, :] * gbuf_ref[HALO - 1:HALO - 1 + tm, :]
            + cw_ref[2:3, :] * gbuf_ref[HALO:HALO + tm, :] + cb_ref[...])
    up = _dot(hn[HALO:HALO + tm], wu_ref[...])
    act = conv * _sigmoid(conv) * up
    acc_ref[...] += _dot(_bf(act), wd_ref[...])

    @pl.when(f == pl.num_programs(2) - 1)
    def _():
        y = x_ref[0] + acc_ref[...]
        if final_norm:
            y = _rms(y, gf_ref[...])
        y_ref[0] = y


def _ffn(x3, g, wg, wu, cw, cb, wd, gf, *, final_norm, tm=512, tf=1408):
    B, S, D = x3.shape
    F = wg.shape[1]
    tm = min(tm, S)
    hb = tm // HALO
    return pl.pallas_call(
        functools.partial(_ffn_kernel, tm=tm, final_norm=final_norm),
        grid=(B, S // tm, F // tf),
        in_specs=[pl.BlockSpec((1, tm, D), lambda b, i, f: (b, i, 0)),
                  pl.BlockSpec((1, HALO, D), lambda b, i, f: (b, jnp.maximum(i * hb - 1, 0), 0)),
                  pl.BlockSpec((1, D), lambda b, i, f: (0, 0)),
                  pl.BlockSpec((D, tf), lambda b, i, f: (0, f)),
                  pl.BlockSpec((D, tf), lambda b, i, f: (0, f)),
                  pl.BlockSpec((3, tf), lambda b, i, f: (0, f)),
                  pl.BlockSpec((1, tf), lambda b, i, f: (0, f)),
                  pl.BlockSpec((tf, D), lambda b, i, f: (f, 0)),
                  pl.BlockSpec((1, D), lambda b, i, f: (0, 0))],
        out_specs=pl.BlockSpec((1, tm, D), lambda b, i, f: (b, i, 0)),
        out_shape=jax.ShapeDtypeStruct((B, S, D), F32),
        scratch_shapes=[pltpu.VMEM((HALO + tm, D), BF16), pltpu.VMEM((HALO + tm, tf), F32),
                        pltpu.VMEM((tm, D), F32)],
        compiler_params=_cparams("parallel", "parallel", "arbitrary"),
    )(x3, x3, g.reshape(1, D), wg, wu, cw, cb.reshape(1, F), wd, gf.reshape(1, D))


def _pad_cols(w, n):
    return jnp.pad(w, ((0, 0), (0, n - w.shape[1])))


def _pad_rows(w, n):
    return jnp.pad(w, ((0, n - w.shape[0]), (0, 0)))


def _swap_halves(w):
    h = w.shape[-1] // 2
    return jnp.concatenate([w[..., h:], w[..., :h]], axis=-1)


def _pack_w1(w_in, v_down):
    D = w_in.shape[0]
    o = 0
    cq = w_in[:, o:o + 256]; o += 256
    ckv = w_in[:, o:o + 128]; o += 128
    kr = w_in[:, o:o + 64]; o += 64
    hg = w_in[:, o:o + 2048]; o += 2048
    su = w_in[:, o:o + 512]; o += 512
    rw = w_in[:, o:o + 1792]; o += 1792
    gates = w_in[:, o:o + 4096]
    rw_p = jnp.concatenate([rw[:, 0:1536], _pad_cols(rw[:, 1536:1600], 128), _pad_cols(rw[:, 1600:1664], 128),
                            rw[:, 1664:1792],
                            _pad_cols(v_down, 128) if v_down is not None else jnp.zeros((D, 128), w_in.dtype)],
                           axis=1)
    mla = jnp.concatenate([cq, ckv, kr, _swap_halves(kr)], axis=1)
    w1 = jnp.concatenate([hg, rw_p, mla, su], axis=1)
    assert w1.shape[1] == P_WIDTH
    return _bf(w1), _bf(gates)


def _pack_mla(w_uq, w_ukv):
    H = MLA_HEADS
    wq = w_uq.reshape(MLA_Q_RANK, H, MLA_QK)
    nope, pe = wq[..., :MLA_NOPE], wq[..., MLA_NOPE:]
    z = jnp.zeros((MLA_Q_RANK, H, 64), w_uq.dtype)
    wq_p = jnp.concatenate([nope, pe, z, _swap_halves(pe), z], axis=-1).reshape(MLA_Q_RANK, H * 384)
    return _bf(wq_p), _bf(w_ukv)


def kernel(x, mem, positions, norm_mix, w_in, mla_q_norm, mla_w_uq, mla_kv_norm, mla_w_ukv, hgrn_lb_logits, hgrn_o_norm, s5_A_re, s5_A_im, s5_log_step, s5_B_re, s5_B_im, s5_C_re, s5_C_im, s5_D, s5_w_glu, s5_b_glu, rwkv_mu, rwkv_w0, rwkv_w_up, rwkv_a0, rwkv_a_up, rwkv_g_up, rwkv_k_k, rwkv_k_a, rwkv_r_k, rwkv_ln_w, rwkv_ln_b, rwkv_vres_down, rwkv_vres_up, rwkv_vres_bias, w_branch_mla, w_branch_hgrn, w_branch_s5, w_branch_rwkv, w_out, norm_xq, norm_xm, xattn_w_q, xattn_w_kv, xattn_w_o, norm_ffn, ffn_w_gate_up, ffn_conv_w, ffn_conv_b, ffn_w_down, norm_final):
    B, S, D = x.shape
    T = B * S
    depth = norm_mix.shape[0]
    M = mem.shape[1]

    lb_p = jax.nn.softmax(hgrn_lb_logits.astype(F32), axis=0)
    lb_c = jnp.cumsum(lb_p, axis=0)
    lower_bounds = lb_c - lb_c[0:1]

    half = MLA_ROPE // 2
    inv_freq = ROPE_THETA ** (-np.arange(half, dtype=np.float32) / half)
    frq = jnp.asarray(np.concatenate([inv_freq, inv_freq])[None, :], F32)
    sgn = jnp.asarray(np.concatenate([-np.ones(half), np.ones(half)])[None, :], F32)
    pos3 = positions.astype(F32).reshape(B, S, 1)

    nsteps = int(math.log2(S // S5_CHUNK))
    mem2 = mem.reshape(B * M, D)
    x2 = x.reshape(T, D)
    v_first = None
    for l in range(depth):
        w1, wgates = _pack_w1(w_in[l], rwkv_vres_down[l - 1] if l > 0 else None)
        p = _norm_proj(x2, norm_mix[l], w1, tn=1280)
        gates = _norm_proj(x2, norm_mix[l], wgates, act="sigmoid", out_dtype=BF16)
        p3 = p.reshape(B, S, P_WIDTH)

        wq_p, wkv_p = _pack_mla(mla_w_uq[l], mla_w_ukv[l])
        q, k, v = _mla_prep(p3, pos3, frq, sgn, mla_q_norm[l].reshape(1, -1), wq_p,
                            mla_kv_norm[l].reshape(1, -1), wkv_p)
        o_mla = _mla_attn(q, k, v).reshape(T, -1)

        o_hg = _hgrn(p3, lower_bounds[l], hgrn_o_norm[l]).reshape(T, -1)

        s5_tabs = _s5_tables(s5_A_re[l], s5_A_im[l], s5_log_step[l], s5_B_re[l], s5_B_im[l],
                             s5_C_re[l], s5_C_im[l], nsteps)
        o_s5 = _s5(p, B, S, s5_tabs, s5_D[l], s5_w_glu[l], s5_b_glu[l])

        mu = rwkv_mu[l]
        mu_p = jnp.concatenate([mu[0:1536], jnp.pad(mu[1536:1600], (0, 64)), jnp.pad(mu[1600:1664], (0, 64)),
                                mu[1664:1792], jnp.zeros((128,), mu.dtype)])
        prm = dict(mu=mu_p, w0=rwkv_w0[l], w_up=_bf(_pad_rows(rwkv_w_up[l], 128)), a0=rwkv_a0[l],
                   a_up=_bf(_pad_rows(rwkv_a_up[l], 128)), g_up=_bf(rwkv_g_up[l]), k_k=rwkv_k_k[l],
                   k_a=rwkv_k_a[l], r_k=rwkv_r_k[l], ln_w=rwkv_ln_w[l], ln_b=rwkv_ln_b[l])
        if l > 0:
            prm["v_up"] = _bf(_pad_rows(rwkv_vres_up[l - 1], 128))
            prm["v_bias"] = rwkv_vres_bias[l - 1]
        o_rw, v_first = _rwkv(p3, v_first, prm)
        o_rw = o_rw.reshape(T, -1)

        wb = _bf(jnp.stack([w_branch_mla[l], w_branch_hgrn[l], w_branch_s5[l], w_branch_rwkv[l]]))
        x2 = _merge(x2, (o_mla, o_hg, o_s5, o_rw), gates, wb, _bf(w_out[l]))

        kv = _norm_proj(mem2, norm_xm[l], _bf(xattn_w_kv[l]), out_dtype=BF16)
        x3 = _xattn(x2.reshape(B, S, D), kv.reshape(B, M, 2 * D), norm_xq[l], _bf(xattn_w_q[l]),
                    _bf(xattn_w_o[l]))

        wgu = ffn_w_gate_up[l]
        x3 = _ffn(x3, norm_ffn[l], _bf(wgu[:, :D_FF]), _bf(wgu[:, D_FF:]), ffn_conv_w[l], ffn_conv_b[l],
                  _bf(ffn_w_down[l]), norm_final, final_norm=(l == depth - 1))
        x2 = x3.reshape(T, D)
    return x2.reshape(B, S, D)
```

```python
import functools
import math

import numpy as np
import jax
import jax.numpy as jnp
from jax import lax
from jax.experimental import pallas as pl
from jax.experimental.pallas import tpu as pltpu

F32 = jnp.float32
BF16 = jnp.bfloat16

V7X_VMEM_LIMIT_BYTES = 56 * 1024 * 1024
LANES = 128

EPS = 1e-6
MLA_HEADS = 4
MLA_Q_RANK = 256
MLA_KV_RANK = 128
MLA_NOPE = 128
MLA_ROPE = 64
MLA_V = 128
MLA_QK = MLA_NOPE + MLA_ROPE
ROPE_THETA = 10000.0
HG_HEADS = 4
HG_DK = 128
HG_W = HG_HEADS * HG_DK
S5_WIDTH = 512
S5_GROUP = 16
S5_GROUPS = S5_WIDTH // S5_GROUP
S5_STATE = 64
S5_CHUNK = 16
S5_TILES = 4
RW_HEADS = 8
RW_HEAD = 64
RW_WIDTH = RW_HEADS * RW_HEAD
RW_GN_EPS = 64e-5
RW_MV_LORA = 32
X_HEADS = 4
D_FF = 2816

P_HG = 0
P_RW = 2048
P_MLA = 4096
P_S5 = 4608
P_WIDTH = 5120

HG_CHUNK = 128
HG_SEL_BELOW = 8
RW_CHUNK = 64
RW_SUBCHUNKS = 4
RW_GROUP_HEADS = 4
RW_GROUP_W = RW_GROUP_HEADS * RW_HEAD


def _cparams(*sem):
    return pltpu.CompilerParams(dimension_semantics=sem, vmem_limit_bytes=V7X_VMEM_LIMIT_BYTES)


def _dot(a, b):
    return jnp.dot(a, b, preferred_element_type=F32)


def _dot_nt(a, b):
    return lax.dot_general(a, b, (((1,), (1,)), ((), ())), preferred_element_type=F32)


def _dot_tn(a, b):
    return lax.dot_general(a, b, (((0,), (0,)), ((), ())), preferred_element_type=F32)


def _bf(x):
    return x.astype(BF16)


def _split2(x):
    hi = x.astype(BF16)
    lo = (x - hi.astype(F32)).astype(BF16)
    return hi, lo


def _split3(x):
    hi = x.astype(BF16)
    r = x - hi.astype(F32)
    mid = r.astype(BF16)
    lo = (r - mid.astype(F32)).astype(BF16)
    return hi, mid, lo


def _dot_exact_lhs(m_bf16, x):
    x0, x1, x2 = _split3(x)
    return _dot(m_bf16, x0) + _dot(m_bf16, x1) + _dot(m_bf16, x2)


def _dot_exact_rhs(x, m_bf16):
    x0, x1 = _split2(x)
    return _dot(x0, m_bf16) + _dot(x1, m_bf16)


def _rms(x, g):
    return x * lax.rsqrt(jnp.mean(x * x, axis=-1, keepdims=True) + EPS) * g


def _sigmoid(x):
    return 1.0 / (1.0 + jnp.exp(-x))


def _full(shape, single=False):
    n = len(shape)
    if single:
        return pl.BlockSpec(shape, lambda *_: (0,) * n, pipeline_mode=pl.Buffered(1))
    return pl.BlockSpec(shape, lambda *_: (0,) * n)


def _in_proj_kernel(x_ref, g_ref, w_ref, p_ref, u_ref):
    y = _dot(_bf(_rms(x_ref[...], g_ref[...])), w_ref[...])
    p_ref[...] = y[:, 0:P_S5]
    u_ref[...] = _bf(y[:, P_S5:P_WIDTH])


def _in_proj(x, g, w, *, tm=256):
    M, K = x.shape
    tm = min(tm, M)
    assert M % tm == 0 and w.shape[1] == P_WIDTH
    return pl.pallas_call(
        _in_proj_kernel,
        name="in_proj",
        grid=(M // tm,),
        in_specs=[pl.BlockSpec((tm, K), lambda i: (i, 0)), _full((1, K)), _full((K, P_WIDTH))],
        out_specs=[pl.BlockSpec((tm, P_S5), lambda i: (i, 0)), pl.BlockSpec((tm, S5_WIDTH), lambda i: (i, 0))],
        out_shape=[jax.ShapeDtypeStruct((M, P_S5), F32), jax.ShapeDtypeStruct((M, S5_WIDTH), BF16)],
        compiler_params=_cparams("parallel"),
    )(x, g.reshape(1, K), w)


def _norm_proj_kernel(x_ref, g_ref, w_ref, o_ref):
    o_ref[...] = _dot(_bf(_rms(x_ref[...], g_ref[...])), w_ref[...]).astype(o_ref.dtype)


def _norm_proj(x, g, w, *, out_dtype=F32, tm=256, name="norm_proj"):
    M, K = x.shape
    N = w.shape[1]
    tm = min(tm, M)
    assert M % tm == 0
    return pl.pallas_call(
        _norm_proj_kernel,
        name=name,
        grid=(M // tm,),
        in_specs=[pl.BlockSpec((tm, K), lambda i: (i, 0)), _full((1, K)), _full((K, N))],
        out_specs=pl.BlockSpec((tm, N), lambda i: (i, 0)),
        out_shape=jax.ShapeDtypeStruct((M, N), out_dtype),
        compiler_params=_cparams("parallel"),
    )(x, g.reshape(1, K), w)


def _mla_prep_kernel(p_ref, pos_ref, frq_ref, sgn_ref, qn_ref, wq_ref, kvn_ref, wkv_ref,
                     q_ref, k_ref, v_ref):
    p = p_ref[0]
    cq = p[:, 0:MLA_Q_RANK]
    ckv = p[:, MLA_Q_RANK:MLA_Q_RANK + MLA_KV_RANK]
    kr = p[:, 384:448]
    kr_sw = p[:, 448:512]
    ang = pos_ref[0] * frq_ref[...]
    cc = jnp.cos(ang)
    ss = jnp.sin(ang) * sgn_ref[...]
    qa = _dot(_bf(_rms(cq, qn_ref[...])), wq_ref[...])
    kva = _dot(_bf(_rms(ckv, kvn_ref[...])), wkv_ref[...])
    kpe = _bf(kr * cc + kr_sw * ss)
    scale = MLA_QK ** -0.5
    for h in range(MLA_HEADS):
        qb = h * 384
        q_pe = qa[:, qb + 128:qb + 192] * cc + qa[:, qb + 256:qb + 320] * ss
        q_ref[0, h, :, 0:MLA_NOPE] = _bf(qa[:, qb:qb + 128] * scale)
        q_ref[0, h, :, MLA_NOPE:MLA_QK] = _bf(q_pe * scale)
        kb = h * 256
        k_ref[0, h, :, 0:MLA_NOPE] = _bf(kva[:, kb:kb + 128])
        k_ref[0, h, :, MLA_NOPE:MLA_QK] = kpe
        v_ref[0, h] = _bf(kva[:, kb + 128:kb + 256])


def _mla_prep(p3, pos3, frq, sgn, qn, wq, kvn, wkv, *, tm=512):
    B, S, _ = p3.shape
    tm = min(tm, S)
    H = MLA_HEADS
    return pl.pallas_call(
        _mla_prep_kernel,
        name="mla_prep",
        grid=(B, S // tm),
        in_specs=[pl.BlockSpec((1, tm, 512), lambda b, i: (b, i, P_MLA // 512)),
                  pl.BlockSpec((1, tm, 1), lambda b, i: (b, i, 0)),
                  _full((1, MLA_ROPE)), _full((1, MLA_ROPE)),
                  _full((1, MLA_Q_RANK)), _full(wq.shape),
                  _full((1, MLA_KV_RANK)), _full(wkv.shape)],
        out_specs=[pl.BlockSpec((1, H, tm, MLA_QK), lambda b, i: (b, 0, i, 0)),
                   pl.BlockSpec((1, H, tm, MLA_QK), lambda b, i: (b, 0, i, 0)),
                   pl.BlockSpec((1, H, tm, MLA_V), lambda b, i: (b, 0, i, 0))],
        out_shape=[jax.ShapeDtypeStruct((B, H, S, MLA_QK), BF16),
                   jax.ShapeDtypeStruct((B, H, S, MLA_QK), BF16),
                   jax.ShapeDtypeStruct((B, H, S, MLA_V), BF16)],
        compiler_params=_cparams("parallel", "parallel"),
    )(p3, pos3, frq, sgn, qn, wq, kvn, wkv)


def _mla_attn_kernel(q_ref, k_ref, v_ref, o_ref, *, tq, hb):
    i = pl.program_id(2)
    qs = [q_ref[0, h] for h in range(hb)]

    def step(h, k, v, carry, mask):
        m, l, acc = carry
        s = _dot_nt(qs[h], k)
        if mask is not None:
            s = jnp.where(mask, s, -1e30)
        m_new = jnp.maximum(m, jnp.max(s, axis=-1, keepdims=True))
        alpha = jnp.exp(m - m_new)
        pr = jnp.exp(s - m_new)
        l = alpha * l + jnp.sum(pr, axis=-1, keepdims=True)
        acc = alpha * acc + _dot(_bf(pr), v)
        return m_new, l, acc

    def blocks(off, carries, mask):
        return tuple(step(h, k_ref[0, h, pl.ds(off, tq), :], v_ref[0, h, pl.ds(off, tq), :], carries[h], mask)
                     for h in range(hb))

    init = tuple((jnp.full((tq, 1), -1e30, F32), jnp.zeros((tq, 1), F32), jnp.zeros((tq, MLA_V), F32))
                 for _ in range(hb))
    carries = lax.fori_loop(0, i, lambda j, c: blocks(pl.multiple_of(j * tq, tq), c, None), init)
    row = lax.broadcasted_iota(jnp.int32, (tq, tq), 0)
    col = lax.broadcasted_iota(jnp.int32, (tq, tq), 1)
    carries = blocks(pl.multiple_of(i * tq, tq), carries, col <= row)
    for h in range(hb):
        _, l, acc = carries[h]
        o_ref[0, :, h * MLA_V:(h + 1) * MLA_V] = _bf(acc / l)


def _mla_attn(q, k, v, *, tq=512, hb=2):
    B, H, S, _ = q.shape
    tq = min(tq, S)
    return pl.pallas_call(
        functools.partial(_mla_attn_kernel, tq=tq, hb=hb),
        name="mla_attn",
        grid=(B, H // hb, S // tq),
        in_specs=[pl.BlockSpec((1, hb, tq, MLA_QK), lambda b, h, i: (b, h, i, 0)),
                  pl.BlockSpec((1, hb, S, MLA_QK), lambda b, h, i: (b, h, 0, 0)),
                  pl.BlockSpec((1, hb, S, MLA_V), lambda b, h, i: (b, h, 0, 0))],
        out_specs=pl.BlockSpec((1, tq, hb * MLA_V), lambda b, h, i: (b, i, h)),
        out_shape=jax.ShapeDtypeStruct((B, S, H * MLA_V), BF16),
        compiler_params=_cparams("parallel", "parallel", "parallel"),
    )(q, k, v)


def _hgrn_tables(C):
    t = np.arange(C)
    nlev = int(math.log2(C))
    masks, sels, sgns = [], [], []
    for lev in range(nlev):
        n = C >> (lev + 1)
        blk2 = t // (2 * n)
        upper = (t // n) % 2
        masks.append((blk2[:, None] == blk2[None, :]) & (upper[:, None] == 1) & (upper[None, :] == 0))
        if n < HG_SEL_BELOW:
            sel = np.zeros((C, C), np.float32)
            sel[t, blk2 * 2 * n + n - 1] = 1.0
            sels.append(sel)
        sgns.append(np.broadcast_to(np.where(upper == 1, 1.0, -1.0)[:, None], (C, HG_DK)))
    masks.append(np.eye(C, dtype=bool))
    tri = (t[:, None] >= t[None, :]).astype(np.float32)
    return (jnp.asarray(np.stack(masks), F32), jnp.asarray(np.concatenate(sels, 0), BF16),
            jnp.asarray(np.stack(sgns), F32), jnp.asarray(tri, BF16))


def _hgrn_kernel(p_ref, lb_ref, onorm_ref, mask_ref, sel_ref, sgn_ref, tri_ref, o_ref, st_ref, *, C):
    @pl.when(pl.program_id(1) == 0)
    def _():
        st_ref[...] = jnp.zeros_like(st_ref)

    nlev = mask_ref.shape[0] - 1
    x = p_ref[0]
    q = x[:, 0:HG_W]
    fc = x[:, HG_W:2 * HG_W]
    v = x[:, 2 * HG_W:3 * HG_W]
    g = x[:, 3 * HG_W:4 * HG_W]
    lb = lb_ref[...]
    f = lb + (1.0 - lb) * _sigmoid(fc)
    logf = jnp.log(f)
    kk = 1.0 - f
    b = _dot_exact_lhs(tri_ref[...], logf)
    b0, b1 = _split2(b)
    bsel = _dot(sel_ref[...], b0) + _dot(sel_ref[...], b1)
    bmids, nsel = [], 0
    for lev in range(nlev):
        n = C >> (lev + 1)
        if n >= HG_SEL_BELOW:
            parts = [jnp.broadcast_to(b[j * 2 * n + n - 1:j * 2 * n + n, :], (2 * n, HG_W)) for j in range(C // (2 * n))]
            bmids.append(parts[0] if len(parts) == 1 else jnp.concatenate(parts, axis=0))
        else:
            bmids.append(bsel[nsel * C:(nsel + 1) * C, :])
            nsel += 1
    b_last = b[C - 1:C, :]
    qdec = q * jnp.exp(b)
    kdec = kk * jnp.exp(b_last - b)
    hs = range(HG_HEADS)
    sls = [slice(h * HG_DK, (h + 1) * HG_DK) for h in hs]
    att = [_dot_nt(_bf(q[:, sl]), _bf(kk[:, sl])) * mask_ref[nlev] for sl in sls]
    for lev in range(nlev):
        es = [jnp.exp(sgn_ref[lev] * (b[:, sl] - bmids[lev][:, sl])) for sl in sls]
        att = [att[h] + _dot_nt(_bf(q[:, sls[h]] * es[h]), _bf(kk[:, sls[h]] * es[h])) * mask_ref[lev] for h in hs]
    sts = [st_ref[h] for h in hs]
    vhs = [_bf(v[:, sl]) for sl in sls]
    os_ = [_dot(_bf(att[h]), vhs[h]) + _dot_nt(_bf(qdec[:, sls[h]]), _bf(sts[h])) for h in hs]
    for h in hs:
        sl = sls[h]
        st_ref[h] = sts[h] * jnp.exp(b_last[:, sl]) + _dot_tn(vhs[h], _bf(kdec[:, sl]))
        o = os_[h]
        o = o * lax.rsqrt(jnp.mean(o * o, axis=-1, keepdims=True) + EPS) * onorm_ref[:, sl]
        o_ref[0, :, sl] = _bf(o * _sigmoid(g[:, sl]))


def _hgrn(p3, lb, onorm, *, C=HG_CHUNK):
    B, S, _ = p3.shape
    C = min(C, S)
    masks, sel, sgn, tri = _hgrn_tables(C)
    return pl.pallas_call(
        functools.partial(_hgrn_kernel, C=C),
        name="hgrn2",
        grid=(B, S // C),
        in_specs=[pl.BlockSpec((1, C, 4 * HG_W), lambda b, i: (b, i, P_HG // (4 * HG_W))),
                  _full((1, HG_W)), _full((1, HG_W)),
                  _full(masks.shape), _full(sel.shape), _full(sgn.shape), _full(tri.shape)],
        out_specs=pl.BlockSpec((1, C, HG_W), lambda b, i: (b, i, 0)),
        out_shape=jax.ShapeDtypeStruct((B, S, HG_W), BF16),
        scratch_shapes=[pltpu.VMEM((HG_HEADS, HG_DK, HG_DK), F32)],
        compiler_params=_cparams("parallel", "arbitrary"),
    )(p3, lb.reshape(1, HG_W), onorm.reshape(1, HG_W), masks, sel, sgn, tri)


def _s5_tables(A_re, A_im, log_step, B_re, B_im, C_re, C_im, nsteps):
    L, G, N, W = S5_CHUNK, S5_GROUPS, S5_STATE, S5_GROUP
    J, GP = S5_TILES, S5_GROUPS // S5_TILES
    hp = lax.Precision.HIGHEST
    a_re = jnp.minimum(A_re.astype(F32), -1e-4)
    a_im = A_im.astype(F32)
    dt = jnp.exp(log_step.astype(F32))[:, None]
    lam_re, lam_im = dt * a_re, dt * a_im

    def apow(m):
        m = jnp.asarray(m, F32).reshape(-1, 1, 1)
        mag = jnp.exp(m * lam_re)
        return mag * jnp.cos(m * lam_im), mag * jnp.sin(m * lam_im)

    pw_re, pw_im = apow(np.arange(L + 1))
    ab_re, ab_im = pw_re[1], pw_im[1]
    den = a_re * a_re + a_im * a_im
    z_re = ((ab_re - 1.0) * a_re + ab_im * a_im) / den
    z_im = (ab_im * a_re - (ab_re - 1.0) * a_im) / den
    Br, Bi = B_re.astype(F32), B_im.astype(F32)
    bb_re = z_re[..., None] * Br - z_im[..., None] * Bi
    bb_im = z_re[..., None] * Bi + z_im[..., None] * Br
    Cr, Ci = C_re.astype(F32), C_im.astype(F32)
    eye = jnp.eye(GP, dtype=F32)

    ca_re = Cr[None] * pw_re[:L, :, None, :] - Ci[None] * pw_im[:L, :, None, :]
    ca_im = Cr[None] * pw_im[:L, :, None, :] + Ci[None] * pw_re[:L, :, None, :]
    kern = (jnp.einsum('lgcn,gnd->lgcd', ca_re, bb_re, precision=hp)
            - jnp.einsum('lgcn,gnd->lgcd', ca_im, bb_im, precision=hp))
    d_i = np.arange(L // 2)[:, None, None]
    la_i = np.arange(2)[None, :, None]
    lb_i = np.arange(2)[None, None, :]
    lag = 2 * d_i + lb_i - la_i
    tt = jnp.where((lag >= 0)[..., None, None, None], kern[np.clip(lag, 0, L - 1)], 0.0)
    tt = tt.reshape(L // 2, 2, 2, J, GP, W, W).transpose(0, 3, 1, 4, 6, 2, 5)
    tt = tt[:, :, :, :, :, :, None, :] * eye[None, None, None, :, None, None, :, None]
    tt = tt.reshape(L // 2, J, 2 * GP * W, 2 * GP * W)

    s_re = jnp.stack([ab_re[..., None] * bb_re - ab_im[..., None] * bb_im, bb_re])
    s_im = jnp.stack([ab_re[..., None] * bb_im + ab_im[..., None] * bb_re, bb_im])
    sb = jnp.stack([s_re, s_im])
    sb = sb.reshape(2, 2, J, GP, N, W).transpose(2, 1, 3, 5, 0, 4)
    sb = sb[:, :, :, :, :, None, :] * eye[None, None, :, None, None, :, None]
    sb = sb.reshape(J, 2 * GP * W, 2 * GP * N)

    o_re = jnp.stack([Cr, ca_re[1]])
    o_im = jnp.stack([Ci, ca_im[1]])
    cc = jnp.stack([o_re, -o_im])
    cc = cc.reshape(2, 2, J, GP, W, N).transpose(2, 0, 3, 5, 1, 4)
    cc = cc[:, :, :, :, :, None, :] * eye[None, None, :, None, None, :, None]
    cc = cc.reshape(J, 2 * GP * N, 2 * GP * W)

    sp_re, sp_im = apow(L * (2 ** np.arange(nsteps)))
    a12 = jnp.stack([pw_re[1], pw_im[1], pw_re[2], pw_im[2]]).reshape(4, G * N)
    return (tt.astype(BF16), sb.astype(BF16), cc.astype(BF16), a12,
            sp_re.reshape(nsteps, G * N), sp_im.reshape(nsteps, G * N))


def _s5_kernel(u_ref, tt_ref, sb_ref, cc_ref, a12_ref, spre_ref, spim_ref, d_ref, wg_ref, bg_ref, o_ref,
               xp_ref, are_ref, aim_ref, bre_ref, bim_ref, ys_ref, *, K, pad, nsteps):
    L, J = S5_CHUNK, S5_TILES
    NP = L // 2
    TW = S5_WIDTH // J
    SW = S5_GROUPS * S5_STATE // J

    for l in range(L):
        for j in range(J):
            xp_ref[l // 2, j, :, (l % 2) * TW:(l % 2 + 1) * TW] = u_ref[0, :, l * S5_WIDTH + j * TW:l * S5_WIDTH + (j + 1) * TW]

    bufs = ((are_ref, aim_ref), (bre_ref, bim_ref))
    (fre, fim), (dre, dim) = bufs[nsteps % 2], bufs[(nsteps + 1) % 2]
    zeros = jnp.zeros((pad, SW), F32)

    def state_tile(j, carry):
        for r in (are_ref, aim_ref, bre_ref, bim_ref):
            r[j, 0:pad, :] = zeros
        a1r, a1i, a2r, a2i = a12_ref[j, 0:1, :], a12_ref[j, 1:2, :], a12_ref[j, 2:3, :], a12_ref[j, 3:4, :]
        acc_re = acc_im = None
        for a in range(NP):
            bu = _dot(xp_ref[a, j], sb_ref[j])
            if a == 0:
                acc_re, acc_im = bu[:, :SW], bu[:, SW:]
            else:
                acc_re, acc_im = (a2r * acc_re - a2i * acc_im + bu[:, :SW],
                                  a2r * acc_im + a2i * acc_re + bu[:, SW:])
        are_ref[j, pad:pad + K, :] = acc_re
        aim_ref[j, pad:pad + K, :] = acc_im
        for s in range(nsteps):
            d = 1 << s
            (sre, sim), (tre, tim) = bufs[s % 2], bufs[(s + 1) % 2]
            pr_, pi_ = spre_ref[j, s:s + 1, :], spim_ref[j, s:s + 1, :]
            cre, cim = sre[j, pad:pad + K, :], sim[j, pad:pad + K, :]
            hre, him = sre[j, pad - d:pad - d + K, :], sim[j, pad - d:pad - d + K, :]
            tre[j, pad:pad + K, :] = cre + pr_ * hre - pi_ * him
            tim[j, pad:pad + K, :] = cim + pr_ * him + pi_ * hre
        xre, xim = fre[j, pad - 1:pad - 1 + K, :], fim[j, pad - 1:pad - 1 + K, :]
        dre[j, pad:pad + K, :] = a1r * xre - a1i * xim
        dim[j, pad:pad + K, :] = a1r * xim + a1i * xre
        for b in range(NP):
            cre, cim = dre[j, pad:pad + K, :], dim[j, pad:pad + K, :]
            acc = _dot(jnp.concatenate([_bf(cre), _bf(cim)], axis=1), cc_ref[j])
            for a in range(b + 1):
                acc = acc + _dot(xp_ref[a, j], tt_ref[b - a, j])
            ys_ref[b, j] = acc
            if b + 1 < NP:
                dre[j, pad:pad + K, :] = a2r * cre - a2i * cim
                dim[j, pad:pad + K, :] = a2r * cim + a2i * cre
        return carry

    lax.fori_loop(0, J, state_tile, 0)

    def pair(b, carry):
        for lb in range(2):
            y = jnp.concatenate([ys_ref[b, j, :, lb * TW:(lb + 1) * TW] for j in range(J)], axis=1)
            u = jnp.concatenate([xp_ref[b, j, :, lb * TW:(lb + 1) * TW] for j in range(J)], axis=1)
            y = y + d_ref[...] * u.astype(F32)
            z = 0.5 * y * (1.0 + jnp.tanh(math.sqrt(2.0 / math.pi) * (y + 0.044715 * (y * y * y))))
            o_ref[0, b, :, lb * S5_WIDTH:(lb + 1) * S5_WIDTH] = _bf(
                z * _sigmoid(_dot(_bf(z), wg_ref[...]) + bg_ref[...]))
        return carry

    lax.fori_loop(0, NP, pair, 0)


def _s5(ub, tabs, D, w_glu, b_glu):
    B, S, _ = ub.shape
    L, J = S5_CHUNK, S5_TILES
    NP = L // 2
    K = S // L
    tt, sb, cc, a12, sp_re, sp_im = tabs
    nsteps = sp_re.shape[0]
    pad = max(8, K // 2)
    SW = S5_GROUPS * S5_STATE // J
    tile_major = lambda t: t.reshape(t.shape[0], J, SW).transpose(1, 0, 2)
    small = [tt, sb, cc, tile_major(a12), tile_major(sp_re), tile_major(sp_im),
             D.reshape(1, -1), _bf(w_glu), b_glu.reshape(1, -1)]
    out = pl.pallas_call(
        functools.partial(_s5_kernel, K=K, pad=pad, nsteps=nsteps),
        name="s5",
        grid=(B,),
        in_specs=[pl.BlockSpec((1, K, L * S5_WIDTH), lambda b: (b, 0, 0))]
        + [_full(a.shape, single=True) for a in small],
        out_specs=pl.BlockSpec((1, NP, K, 2 * S5_WIDTH), lambda b: (b, 0, 0, 0)),
        out_shape=jax.ShapeDtypeStruct((B, NP, K, 2 * S5_WIDTH), BF16),
        scratch_shapes=[pltpu.VMEM((NP, J, K, 2 * S5_WIDTH // J), BF16)]
        + [pltpu.VMEM((J, pad + K, SW), F32)] * 4
        + [pltpu.VMEM((NP, J, K, 2 * S5_WIDTH // J), F32)],
        compiler_params=_cparams("parallel"),
    )(ub.reshape(B, K, L * S5_WIDTH), *small)
    return out.transpose(0, 2, 1, 3).reshape(B * S, S5_WIDTH)


def _rwkv_tables(C, nsub):
    GH, HD, GW = RW_GROUP_HEADS, RW_HEAD, RW_GROUP_W
    t = np.arange(C)[:, None]
    s = np.arange(GW)[None, :] % C
    lane = np.arange(GW)
    lanemask = np.zeros((8, GW), np.float32)
    for h in range(GH):
        lanemask[h] = (lane // HD) == h
    bd = ((lane[:, None] // HD) == (lane[None, :] // HD)).astype(np.float32)
    c = np.arange(RW_WIDTH)
    ones = ((c[:, None] // HD) == (c[None, :] // HD)).astype(np.float32)
    r = np.arange(nsub * C)
    tri = ((r[:, None] >= r[None, :]) & ((r[:, None] // C) == (r[None, :] // C))).astype(np.float32)
    return (jnp.asarray((s < t), F32), jnp.asarray((s <= t), F32), jnp.asarray((s == t), F32),
            jnp.asarray(bd), jnp.asarray(lanemask, BF16), jnp.asarray(ones, BF16), jnp.asarray(tri, BF16))


def _rwkv_kernel(*refs, C, nsub, has_vres):
    if has_vres:
        (p_ref, vf_ref, mu_ref, w0_ref, wup_ref, a0_ref, aup_ref, gup_ref, kk_ref, ka_ref, rk_ref,
         lnw_ref, lnb_ref, vup_ref, vb_ref,
         strict_ref, incl_ref, eye_ref, bd_ref, lm_ref, ones_ref, tri_ref, o_ref, st_ref, prev_ref) = refs
    else:
        (p_ref, mu_ref, w0_ref, wup_ref, a0_ref, aup_ref, gup_ref, kk_ref, ka_ref, rk_ref,
         lnw_ref, lnb_ref,
         strict_ref, incl_ref, eye_ref, bd_ref, lm_ref, ones_ref, tri_ref, o_ref, vout_ref, st_ref, prev_ref) = refs

    @pl.when(pl.program_id(1) == 0)
    def _():
        st_ref[...] = jnp.zeros_like(st_ref)
        prev_ref[...] = jnp.zeros_like(prev_ref)

    W = RW_WIDTH
    R = nsub * C
    x = p_ref[0]
    rolled = pltpu.roll(x, 1, axis=0)
    row = lax.broadcasted_iota(jnp.int32, x.shape, 0)
    shifted = jnp.where(row == 0, prev_ref[0:1, :], rolled)
    prev_ref[0:1, :] = x[R - 1:R, :]
    m = x + (shifted - x) * mu_ref[...]
    r, k, v = m[:, 0:W], m[:, W:2 * W], m[:, 2 * W:3 * W]
    wd, ad, gd = m[:, 3 * W:3 * W + 128], m[:, 3 * W + 128:3 * W + 256], m[:, 3 * W + 256:3 * W + 384]
    zw = -(w0_ref[...] + _dot(_bf(jnp.tanh(wd)), wup_ref[...]))
    softplus = jnp.maximum(zw, 0.0) + jnp.log(1.0 + jnp.exp(-jnp.abs(zw)))
    logw = -jnp.exp(-softplus - 0.5)
    lr = _sigmoid(a0_ref[...] + _dot(_bf(ad), aup_ref[...]))
    gate = _dot(_bf(_sigmoid(gd)), gup_ref[...])
    if has_vres:
        vl = x[:, 3 * W + 384:3 * W + 512]
        v = v + (vf_ref[0] - v) * _sigmoid(vb_ref[...] + _dot(_bf(vl), vup_ref[...]))
    else:
        vout_ref[0] = v
    ones = ones_ref[...]
    kn = k * kk_ref[...]
    kn = kn * lax.rsqrt(_dot_exact_rhs(kn * kn, ones) + 1e-12)
    k2 = k * (1.0 + (lr - 1.0) * ka_ref[...])
    cum = _dot_exact_lhs(tri_ref[...], logw)
    p_inv = jnp.exp(-cum)
    a_all = -kn * jnp.exp(cum - logw)
    r_all = r * jnp.exp(cum)
    b_all = kn * lr * p_inv
    k_all = k2 * p_inv
    knlr = kn * lr

    strict, incl, eye, bd = strict_ref[...], incl_ref[...], eye_ref[...], bd_ref[...]
    GW = RW_GROUP_W
    NG = RW_HEADS // RW_GROUP_HEADS
    HC = RW_GROUP_HEADS * C

    def stack(z):
        zb = _bf(z)
        return jnp.concatenate([zb * lm_ref[h:h + 1, :] for h in range(RW_GROUP_HEADS)], axis=0)

    chains = [(sc, gi) for sc in range(nsub) for gi in range(NG)]
    rsl = lambda sc: slice(sc * C, (sc + 1) * C)
    lsl = lambda gi: slice(gi * GW, (gi + 1) * GW)
    ar, vs, n_w, aak, arbk = {}, {}, {}, {}, {}
    for c in chains:
        rs, sl = rsl(c[0]), lsl(c[1])
        ar[c] = _bf(jnp.concatenate([a_all[rs, sl], r_all[rs, sl]], axis=0))
        vs[c] = stack(v[rs, sl])
        ybk = jnp.concatenate([stack(b_all[rs, sl]), stack(k_all[rs, sl])], axis=0)
        p1 = _dot_nt(ar[c], ybk)
        n_w[c] = p1[0:C, 0:HC] * strict
        aak[c] = _bf(p1[0:C, HC:2 * HC] * strict)
        arbk[c] = _bf(jnp.concatenate([p1[C:2 * C, 0:HC] * incl, p1[C:2 * C, HC:2 * HC] * incl], axis=1))
    tinv = {c: eye + n_w[c] for c in chains}
    pw = dict(n_w)
    for _ in range(int(math.log2(C)) - 1):
        pw = {c: _dot(_bf(pw[c]), stack(pw[c])) for c in chains}
        tinv = {c: tinv[c] + _dot(_bf(tinv[c]), stack(pw[c])) for c in chains}
    zv = {c: _dot(aak[c], vs[c]) for c in chains}
    tinv = {c: _bf(tinv[c]) for c in chains}

    states = [st_ref[gi] for gi in range(NG)]
    out_rows = []
    for sc in range(nsub):
        rs = rsl(sc)
        cum_last = cum[(sc + 1) * C - 1:(sc + 1) * C, :]
        p_end = jnp.exp(cum_last - cum[rs])
        p_last = jnp.exp(cum_last)
        bke = {gi: _bf(jnp.concatenate([knlr[rs, lsl(gi)] * p_end[:, lsl(gi)], k2[rs, lsl(gi)] * p_end[:, lsl(gi)]], axis=0))
               for gi in range(NG)}
        zr = {gi: _dot_nt(ar[(sc, gi)], _bf(states[gi])) for gi in range(NG)}
        u = {gi: _dot(tinv[(sc, gi)], stack(zr[gi][0:C] + zv[(sc, gi)])) for gi in range(NG)}
        outs = [zr[gi][C:2 * C] + _dot(arbk[(sc, gi)], jnp.concatenate([stack(u[gi]), vs[(sc, gi)]], axis=0))
                for gi in range(NG)]
        for gi in range(NG):
            upd = _dot_tn(_bf(jnp.concatenate([u[gi], v[rs, lsl(gi)]], axis=0)), bke[gi])
            states[gi] = states[gi] * p_last[:, lsl(gi)] + upd * bd
        out_rows.append(jnp.concatenate(outs, axis=1))
    for gi in range(NG):
        st_ref[gi] = states[gi]
    y = jnp.concatenate(out_rows, axis=0)
    inv_n = 1.0 / RW_HEAD
    mean = _dot_exact_rhs(y, ones) * inv_n
    yc = y - mean
    var = _dot_exact_rhs(yc * yc, ones) * inv_n
    yn = yc * lax.rsqrt(var + RW_GN_EPS) * lnw_ref[...] + lnb_ref[...]
    bonus = _dot_exact_rhs(r * k2 * rk_ref[...], ones) * v
    o_ref[0] = _bf((yn + bonus) * gate)


def _rwkv(p3, v_first, prm, *, C=RW_CHUNK, nsub=RW_SUBCHUNKS):
    B, S, _ = p3.shape
    C = min(C, S)
    R = nsub * C
    assert S % R == 0
    has_vres = v_first is not None
    tabs = _rwkv_tables(C, nsub)
    W = RW_WIDTH
    NG = RW_HEADS // RW_GROUP_HEADS
    row = lambda a: a.reshape(1, -1)
    args = [p3]
    specs = [pl.BlockSpec((1, R, 2048), lambda b, i: (b, i, P_RW // 2048))]
    if has_vres:
        args.append(v_first)
        specs.append(pl.BlockSpec((1, R, W), lambda b, i: (b, i, 0)))
    small = [row(prm["mu"]), row(prm["w0"]), prm["w_up"], row(prm["a0"]), prm["a_up"], prm["g_up"],
             row(prm["k_k"]), row(prm["k_a"]), row(prm["r_k"]), row(prm["ln_w"]), row(prm["ln_b"])]
    if has_vres:
        small += [prm["v_up"], row(prm["v_bias"])]
    small += list(tabs)
    args += small
    specs += [_full(a.shape) for a in small]
    o_spec = pl.BlockSpec((1, R, W), lambda b, i: (b, i, 0))
    if has_vres:
        out_specs, out_shape = o_spec, jax.ShapeDtypeStruct((B, S, W), BF16)
    else:
        out_specs = [o_spec, o_spec]
        out_shape = [jax.ShapeDtypeStruct((B, S, W), BF16), jax.ShapeDtypeStruct((B, S, W), F32)]
    res = pl.pallas_call(
        functools.partial(_rwkv_kernel, C=C, nsub=nsub, has_vres=has_vres),
        name="rwkv7",
        grid=(B, S // R),
        in_specs=specs, out_specs=out_specs, out_shape=out_shape,
        scratch_shapes=[pltpu.VMEM((NG, RW_GROUP_W, RW_GROUP_W), F32), pltpu.VMEM((8, 2048), F32)],
        compiler_params=_cparams("parallel", "arbitrary"),
    )(*args)
    return (res, v_first) if has_vres else (res[0], res[1])


def _merge_kernel(x_ref, o0_ref, o1_ref, o2_ref, o3_ref, g_ref, wg_ref, wb_ref, wo_ref, y_ref):
    x = x_ref[...]
    D = x.shape[1]
    hn = _bf(_rms(x, g_ref[...]))
    acc = None
    for m, o_ref in enumerate((o0_ref, o1_ref, o2_ref, o3_ref)):
        gate = _sigmoid(_dot(hn, wg_ref[:, m * D:(m + 1) * D]))
        t = gate * _dot(o_ref[...], wb_ref[m])
        acc = t if acc is None else acc + t
    y_ref[...] = x + _dot(_bf(acc), wo_ref[...])


def _merge(x, g, outs, wg, wb, wo, *, tm=256):
    T, D = x.shape
    tm = min(tm, T)
    bw = outs[0].shape[1]
    return pl.pallas_call(
        _merge_kernel,
        name="gated_merge",
        grid=(T // tm,),
        in_specs=[pl.BlockSpec((tm, D), lambda i: (i, 0))]
        + [pl.BlockSpec((tm, bw), lambda i: (i, 0))] * 4
        + [_full((1, D)), _full(wg.shape), _full(wb.shape), _full(wo.shape)],
        out_specs=pl.BlockSpec((tm, D), lambda i: (i, 0)),
        out_shape=jax.ShapeDtypeStruct((T, D), F32),
        compiler_params=_cparams("parallel"),
    )(x, *outs, g.reshape(1, D), wg, wb, wo)


def _xattn_kernel(x_ref, kv_ref, g_ref, wq_ref, wo_ref, y_ref):
    x = x_ref[0]
    D = x.shape[1]
    hd = D // X_HEADS
    q = _bf(_dot(_bf(_rms(x, g_ref[...])), wq_ref[...]) * (hd ** -0.5))
    outs = []
    for h in range(X_HEADS):
        kh = kv_ref[0, :, h * hd:(h + 1) * hd]
        vh = kv_ref[0, :, D + h * hd:D + (h + 1) * hd]
        s = _dot_nt(q[:, h * hd:(h + 1) * hd], kh)
        e = jnp.exp(s - jnp.max(s, axis=-1, keepdims=True))
        pr = e / jnp.sum(e, axis=-1, keepdims=True)
        outs.append(_bf(_dot(_bf(pr), vh)))
    y_ref[0] = x + _dot(jnp.concatenate(outs, axis=1), wo_ref[...])


def _xattn(x3, kv3, g, wq, wo, *, tm=512):
    B, S, D = x3.shape
    M = kv3.shape[1]
    tm = min(tm, S)
    return pl.pallas_call(
        _xattn_kernel,
        name="cross_attn",
        grid=(B, S // tm),
        in_specs=[pl.BlockSpec((1, tm, D), lambda b, i: (b, i, 0)),
                  pl.BlockSpec((1, M, 2 * D), lambda b, i: (b, 0, 0)),
                  _full((1, D)), _full(wq.shape), _full(wo.shape)],
        out_specs=pl.BlockSpec((1, tm, D), lambda b, i: (b, i, 0)),
        out_shape=jax.ShapeDtypeStruct((B, S, D), F32),
        compiler_params=_cparams("parallel", "parallel"),
    )(x3, kv3, g.reshape(1, D), wq, wo)


HALO = 16


def _ffn_kernel(x_ref, halo_ref, g_ref, wg_ref, wu_ref, cw_ref, cb_ref, wd_ref, gf_ref, y_ref,
                hn_ref, gbuf_ref, acc_ref, *, tm, final_norm):
    i = pl.program_id(1)
    f = pl.program_id(2)

    @pl.when(f == 0)
    def _():
        hn_ref[0:HALO, :] = _bf(_rms(halo_ref[0], g_ref[...]))
        hn_ref[HALO:HALO + tm, :] = _bf(_rms(x_ref[0], g_ref[...]))
        acc_ref[...] = jnp.zeros_like(acc_ref)

    hn = hn_ref[...]
    gpre = _dot(hn, wg_ref[...])
    keep = (i > 0).astype(F32)
    gbuf_ref[0:HALO, :] = gpre[0:HALO] * keep
    gbuf_ref[HALO:HALO + tm, :] = gpre[HALO:HALO + tm]
    conv = (cw_ref[0:1, :] * gbuf_ref[HALO - 2:HALO - 2 + tm, :]
            + cw_ref[1:2, :] * gbuf_ref[HALO - 1:HALO - 1 + tm, :]
            + cw_ref[2:3, :] * gbuf_ref[HALO:HALO + tm, :] + cb_ref[...])
    up = _dot(hn[HALO:HALO + tm], wu_ref[...])
    act = conv * _sigmoid(conv) * up
    acc_ref[...] += _dot(_bf(act), wd_ref[...])

    @pl.when(f == pl.num_programs(2) - 1)
    def _():
        y = x_ref[0] + acc_ref[...]
        if final_norm:
            y = _rms(y, gf_ref[...])
        y_ref[0] = y


def _ffn(x3, g, wg, wu, cw, cb, wd, gf, *, final_norm, tm=512, tf=1408):
    B, S, D = x3.shape
    F = wg.shape[1]
    tm = min(tm, S)
    hb = tm // HALO
    return pl.pallas_call(
        functools.partial(_ffn_kernel, tm=tm, final_norm=final_norm),
        name="conv_ffn",
        grid=(B, S // tm, F // tf),
        in_specs=[pl.BlockSpec((1, tm, D), lambda b, i, f: (b, i, 0)),
                  pl.BlockSpec((1, HALO, D), lambda b, i, f: (b, jnp.maximum(i * hb - 1, 0), 0)),
                  pl.BlockSpec((1, D), lambda b, i, f: (0, 0)),
                  pl.BlockSpec((D, tf), lambda b, i, f: (0, f)),
                  pl.BlockSpec((D, tf), lambda b, i, f: (0, f)),
                  pl.BlockSpec((3, tf), lambda b, i, f: (0, f)),
                  pl.BlockSpec((1, tf), lambda b, i, f: (0, f)),
                  pl.BlockSpec((tf, D), lambda b, i, f: (f, 0)),
                  pl.BlockSpec((1, D), lambda b, i, f: (0, 0))],
        out_specs=pl.BlockSpec((1, tm, D), lambda b, i, f: (b, i, 0)),
        out_shape=jax.ShapeDtypeStruct((B, S, D), F32),
        scratch_shapes=[pltpu.VMEM((HALO + tm, D), BF16), pltpu.VMEM((HALO + tm, tf), F32),
                        pltpu.VMEM((tm, D), F32)],
        compiler_params=_cparams("parallel", "parallel", "arbitrary"),
    )(x3, x3, g.reshape(1, D), wg, wu, cw, cb.reshape(1, F), wd, gf.reshape(1, D))


def _pad_cols(w, n):
    return jnp.pad(w, ((0, 0), (0, n - w.shape[1])))


def _pad_rows(w, n):
    return jnp.pad(w, ((0, n - w.shape[0]), (0, 0)))


def _swap_halves(w):
    h = w.shape[-1] // 2
    return jnp.concatenate([w[..., h:], w[..., :h]], axis=-1)


def _pack_w1(w_in, v_down):
    D = w_in.shape[0]
    w_in = _bf(w_in)
    if v_down is not None:
        v_down = _bf(v_down)
    o = 0
    cq = w_in[:, o:o + 256]; o += 256
    ckv = w_in[:, o:o + 128]; o += 128
    kr = w_in[:, o:o + 64]; o += 64
    hg = w_in[:, o:o + 2048]; o += 2048
    su = w_in[:, o:o + 512]; o += 512
    rw = w_in[:, o:o + 1792]; o += 1792
    gates = w_in[:, o:o + 4096]
    rw_p = jnp.concatenate([rw[:, 0:1536], _pad_cols(rw[:, 1536:1600], 128), _pad_cols(rw[:, 1600:1664], 128),
                            rw[:, 1664:1792],
                            _pad_cols(v_down, 128) if v_down is not None else jnp.zeros((D, 128), w_in.dtype)],
                           axis=1)
    mla = jnp.concatenate([cq, ckv, kr, _swap_halves(kr)], axis=1)
    w1 = jnp.concatenate([hg, rw_p, mla, su], axis=1)
    assert w1.shape[1] == P_WIDTH
    return w1, gates


def _pack_mla(w_uq, w_ukv):
    H = MLA_HEADS
    wq = w_uq.reshape(MLA_Q_RANK, H, MLA_QK)
    nope, pe = wq[..., :MLA_NOPE], wq[..., MLA_NOPE:]
    z = jnp.zeros((MLA_Q_RANK, H, 64), w_uq.dtype)
    wq_p = jnp.concatenate([nope, pe, z, _swap_halves(pe), z], axis=-1).reshape(MLA_Q_RANK, H * 384)
    return _bf(wq_p), _bf(w_ukv)


def kernel(x, mem, positions, norm_mix, w_in, mla_q_norm, mla_w_uq, mla_kv_norm, mla_w_ukv, hgrn_lb_logits, hgrn_o_norm, s5_A_re, s5_A_im, s5_log_step, s5_B_re, s5_B_im, s5_C_re, s5_C_im, s5_D, s5_w_glu, s5_b_glu, rwkv_mu, rwkv_w0, rwkv_w_up, rwkv_a0, rwkv_a_up, rwkv_g_up, rwkv_k_k, rwkv_k_a, rwkv_r_k, rwkv_ln_w, rwkv_ln_b, rwkv_vres_down, rwkv_vres_up, rwkv_vres_bias, w_branch_mla, w_branch_hgrn, w_branch_s5, w_branch_rwkv, w_out, norm_xq, norm_xm, xattn_w_q, xattn_w_kv, xattn_w_o, norm_ffn, ffn_w_gate_up, ffn_conv_w, ffn_conv_b, ffn_w_down, norm_final):
    B, S, D = x.shape
    T = B * S
    depth = norm_mix.shape[0]
    M = mem.shape[1]

    lb_p = jax.nn.softmax(hgrn_lb_logits.astype(F32), axis=0)
    lb_c = jnp.cumsum(lb_p, axis=0)
    lower_bounds = lb_c - lb_c[0:1]

    half = MLA_ROPE // 2
    inv_freq = ROPE_THETA ** (-np.arange(half, dtype=np.float32) / half)
    frq = jnp.asarray(np.concatenate([inv_freq, inv_freq])[None, :], F32)
    sgn = jnp.asarray(np.concatenate([-np.ones(half), np.ones(half)])[None, :], F32)
    pos3 = positions.astype(F32).reshape(B, S, 1)

    nsteps = int(math.log2(S // S5_CHUNK))
    mem2 = mem.reshape(B * M, D)
    x2 = x.reshape(T, D)
    v_first = None
    for l in range(depth):
        w1, wgates = _pack_w1(w_in[l], rwkv_vres_down[l - 1] if l > 0 else None)
        p, ub = _in_proj(x2, norm_mix[l], w1)
        p3 = p.reshape(B, S, P_S5)

        wq_p, wkv_p = _pack_mla(mla_w_uq[l], mla_w_ukv[l])
        q, k, v = _mla_prep(p3, pos3, frq, sgn, mla_q_norm[l].reshape(1, -1), wq_p,
                            mla_kv_norm[l].reshape(1, -1), wkv_p)
        o_mla = _mla_attn(q, k, v).reshape(T, -1)

        o_hg = _hgrn(p3, lower_bounds[l], hgrn_o_norm[l]).reshape(T, -1)

        s5_tabs = _s5_tables(s5_A_re[l], s5_A_im[l], s5_log_step[l], s5_B_re[l], s5_B_im[l],
                             s5_C_re[l], s5_C_im[l], nsteps)
        o_s5 = _s5(ub.reshape(B, S, S5_WIDTH), s5_tabs, s5_D[l], s5_w_glu[l], s5_b_glu[l])

        mu = rwkv_mu[l]
        mu_p = jnp.concatenate([mu[0:1536], jnp.pad(mu[1536:1600], (0, 64)), jnp.pad(mu[1600:1664], (0, 64)),
                                mu[1664:1792], jnp.zeros((128,), mu.dtype)])
        prm = dict(mu=mu_p, w0=rwkv_w0[l], w_up=_bf(_pad_rows(rwkv_w_up[l], 128)), a0=rwkv_a0[l],
                   a_up=_bf(_pad_rows(rwkv_a_up[l], 128)), g_up=_bf(rwkv_g_up[l]), k_k=rwkv_k_k[l],
                   k_a=rwkv_k_a[l], r_k=rwkv_r_k[l], ln_w=rwkv_ln_w[l], ln_b=rwkv_ln_b[l])
        if l > 0:
            prm["v_up"] = _bf(_pad_rows(rwkv_vres_up[l - 1], 128))
            prm["v_bias"] = rwkv_vres_bias[l - 1]
        o_rw, v_first = _rwkv(p3, v_first, prm)
        o_rw = o_rw.reshape(T, -1)

        wb = _bf(jnp.stack([w_branch_mla[l], w_branch_hgrn[l], w_branch_s5[l], w_branch_rwkv[l]]))
        x2 = _merge(x2, norm_mix[l], (o_mla, o_hg, o_s5, o_rw), wgates, wb, _bf(w_out[l]))

        kv = _norm_proj(mem2, norm_xm[l], _bf(xattn_w_kv[l]), out_dtype=BF16, name="xattn_kv_proj")
        x3 = _xattn(x2.reshape(B, S, D), kv.reshape(B, M, 2 * D), norm_xq[l], _bf(xattn_w_q[l]),
                    _bf(xattn_w_o[l]))

        wgu = ffn_w_gate_up[l]
        x3 = _ffn(x3, norm_ffn[l], _bf(wgu[:, :D_FF]), _bf(wgu[:, D_FF:]), ffn_conv_w[l], ffn_conv_b[l],
                  _bf(ffn_w_down[l]), norm_final, final_norm=(l == depth - 1))
        x2 = x3.reshape(T, D)
    return x2.reshape(B, S, D)
```

```python
import functools
import math

import numpy as np
import jax
import jax.numpy as jnp
from jax import lax
from jax.experimental import pallas as pl
from jax.experimental.pallas import tpu as pltpu

F32 = jnp.float32
BF16 = jnp.bfloat16

V7X_VMEM_LIMIT_BYTES = 56 * 1024 * 1024
LANES = 128

EPS = 1e-6
MLA_HEADS = 4
MLA_Q_RANK = 256
MLA_KV_RANK = 128
MLA_NOPE = 128
MLA_ROPE = 64
MLA_V = 128
MLA_QK = MLA_NOPE + MLA_ROPE
ROPE_THETA = 10000.0
HG_HEADS = 4
HG_DK = 128
HG_W = HG_HEADS * HG_DK
S5_WIDTH = 512
S5_GROUP = 16
S5_GROUPS = S5_WIDTH // S5_GROUP
S5_STATE = 64
S5_CHUNK = 16
S5_TILES = 4
RW_HEADS = 8
RW_HEAD = 64
RW_WIDTH = RW_HEADS * RW_HEAD
RW_GN_EPS = 64e-5
RW_MV_LORA = 32
X_HEADS = 4
D_FF = 2816

P_HG = 0
P_RW = 2048
P_MLA = 4096
P_S5 = 4608
P_WIDTH = 5120

HG_CHUNK = 128
HG_SEL_BELOW = 8
RW_CHUNK = 64
RW_SUBCHUNKS = 4
RW_GROUP_HEADS = 4
RW_GROUP_W = RW_GROUP_HEADS * RW_HEAD


def _cparams(*sem):
    return pltpu.CompilerParams(dimension_semantics=sem, vmem_limit_bytes=V7X_VMEM_LIMIT_BYTES)


def _dot(a, b):
    return jnp.dot(a, b, preferred_element_type=F32)


def _dot_nt(a, b):
    return lax.dot_general(a, b, (((1,), (1,)), ((), ())), preferred_element_type=F32)


def _dot_tn(a, b):
    return lax.dot_general(a, b, (((0,), (0,)), ((), ())), preferred_element_type=F32)


def _bf(x):
    return x.astype(BF16)


def _split2(x):
    hi = x.astype(BF16)
    lo = (x - hi.astype(F32)).astype(BF16)
    return hi, lo


def _split3(x):
    hi = x.astype(BF16)
    r = x - hi.astype(F32)
    mid = r.astype(BF16)
    lo = (r - mid.astype(F32)).astype(BF16)
    return hi, mid, lo


def _dot_exact_lhs(m_bf16, x):
    x0, x1, x2 = _split3(x)
    return _dot(m_bf16, x0) + _dot(m_bf16, x1) + _dot(m_bf16, x2)


def _dot_exact_rhs(x, m_bf16):
    x0, x1 = _split2(x)
    return _dot(x0, m_bf16) + _dot(x1, m_bf16)


def _rms(x, g):
    return x * lax.rsqrt(jnp.mean(x * x, axis=-1, keepdims=True) + EPS) * g


def _sigmoid(x):
    return 1.0 / (1.0 + jnp.exp(-x))


def _full(shape, single=False):
    n = len(shape)
    if single:
        return pl.BlockSpec(shape, lambda *_: (0,) * n, pipeline_mode=pl.Buffered(1))
    return pl.BlockSpec(shape, lambda *_: (0,) * n)


def _in_proj_kernel(x_ref, g_ref, w_ref, p_ref, u_ref):
    y = _dot(_bf(_rms(x_ref[...], g_ref[...])), w_ref[...])
    p_ref[...] = y[:, 0:P_S5]
    u_ref[...] = _bf(y[:, P_S5:P_WIDTH])


def _in_proj(x, g, w, *, tm=256):
    M, K = x.shape
    tm = min(tm, M)
    assert M % tm == 0 and w.shape[1] == P_WIDTH
    return pl.pallas_call(
        _in_proj_kernel,
        name="in_proj",
        grid=(M // tm,),
        in_specs=[pl.BlockSpec((tm, K), lambda i: (i, 0)), _full((1, K)), _full((K, P_WIDTH))],
        out_specs=[pl.BlockSpec((tm, P_S5), lambda i: (i, 0)), pl.BlockSpec((tm, S5_WIDTH), lambda i: (i, 0))],
        out_shape=[jax.ShapeDtypeStruct((M, P_S5), F32), jax.ShapeDtypeStruct((M, S5_WIDTH), BF16)],
        compiler_params=_cparams("parallel"),
    )(x, g.reshape(1, K), w)


def _norm_proj_kernel(x_ref, g_ref, w_ref, o_ref):
    o_ref[...] = _dot(_bf(_rms(x_ref[...], g_ref[...])), w_ref[...]).astype(o_ref.dtype)


def _norm_proj(x, g, w, *, out_dtype=F32, tm=256, name="norm_proj"):
    M, K = x.shape
    N = w.shape[1]
    tm = min(tm, M)
    assert M % tm == 0
    return pl.pallas_call(
        _norm_proj_kernel,
        name=name,
        grid=(M // tm,),
        in_specs=[pl.BlockSpec((tm, K), lambda i: (i, 0)), _full((1, K)), _full((K, N))],
        out_specs=pl.BlockSpec((tm, N), lambda i: (i, 0)),
        out_shape=jax.ShapeDtypeStruct((M, N), out_dtype),
        compiler_params=_cparams("parallel"),
    )(x, g.reshape(1, K), w)


def _mla_prep_kernel(p_ref, pos_ref, frq_ref, sgn_ref, qn_ref, wq_ref, kvn_ref, wkv_ref,
                     q_ref, k_ref, v_ref):
    p = p_ref[0]
    cq = p[:, 0:MLA_Q_RANK]
    ckv = p[:, MLA_Q_RANK:MLA_Q_RANK + MLA_KV_RANK]
    kr = p[:, 384:448]
    kr_sw = p[:, 448:512]
    ang = pos_ref[0] * frq_ref[...]
    cc = jnp.cos(ang)
    ss = jnp.sin(ang) * sgn_ref[...]
    qa = _dot(_bf(_rms(cq, qn_ref[...])), wq_ref[...])
    kva = _dot(_bf(_rms(ckv, kvn_ref[...])), wkv_ref[...])
    kpe = _bf(kr * cc + kr_sw * ss)
    scale = MLA_QK ** -0.5
    for h in range(MLA_HEADS):
        qb = h * 384
        q_pe = qa[:, qb + 128:qb + 192] * cc + qa[:, qb + 256:qb + 320] * ss
        q_ref[0, h, :, 0:MLA_NOPE] = _bf(qa[:, qb:qb + 128] * scale)
        q_ref[0, h, :, MLA_NOPE:MLA_QK] = _bf(q_pe * scale)
        kb = h * 256
        k_ref[0, h, :, 0:MLA_NOPE] = _bf(kva[:, kb:kb + 128])
        k_ref[0, h, :, MLA_NOPE:MLA_QK] = kpe
        v_ref[0, h] = _bf(kva[:, kb + 128:kb + 256])


def _mla_prep(p3, pos3, frq, sgn, qn, wq, kvn, wkv, *, tm=512):
    B, S, _ = p3.shape
    tm = min(tm, S)
    H = MLA_HEADS
    return pl.pallas_call(
        _mla_prep_kernel,
        name="mla_prep",
        grid=(B, S // tm),
        in_specs=[pl.BlockSpec((1, tm, 512), lambda b, i: (b, i, P_MLA // 512)),
                  pl.BlockSpec((1, tm, 1), lambda b, i: (b, i, 0)),
                  _full((1, MLA_ROPE)), _full((1, MLA_ROPE)),
                  _full((1, MLA_Q_RANK)), _full(wq.shape),
                  _full((1, MLA_KV_RANK)), _full(wkv.shape)],
        out_specs=[pl.BlockSpec((1, H, tm, MLA_QK), lambda b, i: (b, 0, i, 0)),
                   pl.BlockSpec((1, H, tm, MLA_QK), lambda b, i: (b, 0, i, 0)),
                   pl.BlockSpec((1, H, tm, MLA_V), lambda b, i: (b, 0, i, 0))],
        out_shape=[jax.ShapeDtypeStruct((B, H, S, MLA_QK), BF16),
                   jax.ShapeDtypeStruct((B, H, S, MLA_QK), BF16),
                   jax.ShapeDtypeStruct((B, H, S, MLA_V), BF16)],
        compiler_params=_cparams("parallel", "parallel"),
    )(p3, pos3, frq, sgn, qn, wq, kvn, wkv)


def _mla_attn_kernel(q_ref, k_ref, v_ref, o_ref, *, tq, hb):
    i = pl.program_id(2)
    qs = [q_ref[0, h] for h in range(hb)]

    def step(h, k, v, carry, mask):
        m, l, acc = carry
        s = _dot_nt(qs[h], k)
        if mask is not None:
            s = jnp.where(mask, s, -1e30)
        m_new = jnp.maximum(m, jnp.max(s, axis=-1, keepdims=True))
        alpha = jnp.exp(m - m_new)
        pr = jnp.exp(s - m_new)
        l = alpha * l + jnp.sum(pr, axis=-1, keepdims=True)
        acc = alpha * acc + _dot(_bf(pr), v)
        return m_new, l, acc

    def blocks(off, carries, mask):
        return tuple(step(h, k_ref[0, h, pl.ds(off, tq), :], v_ref[0, h, pl.ds(off, tq), :], carries[h], mask)
                     for h in range(hb))

    init = tuple((jnp.full((tq, 1), -1e30, F32), jnp.zeros((tq, 1), F32), jnp.zeros((tq, MLA_V), F32))
                 for _ in range(hb))
    carries = lax.fori_loop(0, i, lambda j, c: blocks(pl.multiple_of(j * tq, tq), c, None), init)
    row = lax.broadcasted_iota(jnp.int32, (tq, tq), 0)
    col = lax.broadcasted_iota(jnp.int32, (tq, tq), 1)
    carries = blocks(pl.multiple_of(i * tq, tq), carries, col <= row)
    for h in range(hb):
        _, l, acc = carries[h]
        o_ref[0, :, h * MLA_V:(h + 1) * MLA_V] = _bf(acc / l)


def _mla_attn(q, k, v, *, tq=1024, hb=2):
    B, H, S, _ = q.shape
    tq = min(tq, S)
    return pl.pallas_call(
        functools.partial(_mla_attn_kernel, tq=tq, hb=hb),
        name="mla_attn",
        grid=(B, H // hb, S // tq),
        in_specs=[pl.BlockSpec((1, hb, tq, MLA_QK), lambda b, h, i: (b, h, i, 0)),
                  pl.BlockSpec((1, hb, S, MLA_QK), lambda b, h, i: (b, h, 0, 0)),
                  pl.BlockSpec((1, hb, S, MLA_V), lambda b, h, i: (b, h, 0, 0))],
        out_specs=pl.BlockSpec((1, tq, hb * MLA_V), lambda b, h, i: (b, i, h)),
        out_shape=jax.ShapeDtypeStruct((B, S, H * MLA_V), BF16),
        compiler_params=_cparams("parallel", "parallel", "parallel"),
    )(q, k, v)


def _hgrn_tables(C):
    t = np.arange(C)
    nlev = int(math.log2(C))
    masks, sels, sgns = [], [], []
    for lev in range(nlev):
        n = C >> (lev + 1)
        blk2 = t // (2 * n)
        upper = (t // n) % 2
        masks.append((blk2[:, None] == blk2[None, :]) & (upper[:, None] == 1) & (upper[None, :] == 0))
        if n < HG_SEL_BELOW:
            sel = np.zeros((C, C), np.float32)
            sel[t, blk2 * 2 * n + n - 1] = 1.0
            sels.append(sel)
        sgns.append(np.broadcast_to(np.where(upper == 1, 1.0, -1.0)[:, None], (C, HG_DK)))
    masks.append(np.eye(C, dtype=bool))
    tri = (t[:, None] >= t[None, :]).astype(np.float32)
    return (jnp.asarray(np.stack(masks), F32), jnp.asarray(np.concatenate(sels, 0), BF16),
            jnp.asarray(np.stack(sgns), F32), jnp.asarray(tri, BF16))


def _hgrn_kernel(p_ref, lb_ref, onorm_ref, mask_ref, sel_ref, sgn_ref, tri_ref, o_ref, st_ref, *, C):
    @pl.when(pl.program_id(1) == 0)
    def _():
        st_ref[...] = jnp.zeros_like(st_ref)

    nlev = mask_ref.shape[0] - 1
    x = p_ref[0]
    q = x[:, 0:HG_W]
    fc = x[:, HG_W:2 * HG_W]
    v = x[:, 2 * HG_W:3 * HG_W]
    g = x[:, 3 * HG_W:4 * HG_W]
    lb = lb_ref[...]
    f = lb + (1.0 - lb) * _sigmoid(fc)
    logf = jnp.log(f)
    kk = 1.0 - f
    b = _dot_exact_lhs(tri_ref[...], logf)
    b0, b1 = _split2(b)
    bsel = _dot(sel_ref[...], b0) + _dot(sel_ref[...], b1)
    bmids, nsel = [], 0
    for lev in range(nlev):
        n = C >> (lev + 1)
        if n >= HG_SEL_BELOW:
            parts = [jnp.broadcast_to(b[j * 2 * n + n - 1:j * 2 * n + n, :], (2 * n, HG_W)) for j in range(C // (2 * n))]
            bmids.append(parts[0] if len(parts) == 1 else jnp.concatenate(parts, axis=0))
        else:
            bmids.append(bsel[nsel * C:(nsel + 1) * C, :])
            nsel += 1
    b_last = b[C - 1:C, :]
    qdec = q * jnp.exp(b)
    kdec = kk * jnp.exp(b_last - b)
    hs = range(HG_HEADS)
    sls = [slice(h * HG_DK, (h + 1) * HG_DK) for h in hs]
    att = [_dot_nt(_bf(q[:, sl]), _bf(kk[:, sl])) * mask_ref[nlev] for sl in sls]
    for lev in range(nlev):
        es = [jnp.exp(sgn_ref[lev] * (b[:, sl] - bmids[lev][:, sl])) for sl in sls]
        att = [att[h] + _dot_nt(_bf(q[:, sls[h]] * es[h]), _bf(kk[:, sls[h]] * es[h])) * mask_ref[lev] for h in hs]
    sts = [st_ref[h] for h in hs]
    vhs = [_bf(v[:, sl]) for sl in sls]
    os_ = [_dot(_bf(att[h]), vhs[h]) + _dot_nt(_bf(qdec[:, sls[h]]), _bf(sts[h])) for h in hs]
    for h in hs:
        sl = sls[h]
        st_ref[h] = sts[h] * jnp.exp(b_last[:, sl]) + _dot_tn(vhs[h], _bf(kdec[:, sl]))
        o = os_[h]
        o = o * lax.rsqrt(jnp.mean(o * o, axis=-1, keepdims=True) + EPS) * onorm_ref[:, sl]
        o_ref[0, :, sl] = _bf(o * _sigmoid(g[:, sl]))


def _hgrn(p3, lb, onorm, *, C=HG_CHUNK):
    B, S, _ = p3.shape
    C = min(C, S)
    masks, sel, sgn, tri = _hgrn_tables(C)
    return pl.pallas_call(
        functools.partial(_hgrn_kernel, C=C),
        name="hgrn2",
        grid=(B, S // C),
        in_specs=[pl.BlockSpec((1, C, 4 * HG_W), lambda b, i: (b, i, P_HG // (4 * HG_W))),
                  _full((1, HG_W)), _full((1, HG_W)),
                  _full(masks.shape), _full(sel.shape), _full(sgn.shape), _full(tri.shape)],
        out_specs=pl.BlockSpec((1, C, HG_W), lambda b, i: (b, i, 0)),
        out_shape=jax.ShapeDtypeStruct((B, S, HG_W), BF16),
        scratch_shapes=[pltpu.VMEM((HG_HEADS, HG_DK, HG_DK), F32)],
        compiler_params=_cparams("parallel", "arbitrary"),
    )(p3, lb.reshape(1, HG_W), onorm.reshape(1, HG_W), masks, sel, sgn, tri)


def _s5_tables(A_re, A_im, log_step, B_re, B_im, C_re, C_im, nsteps):
    L, G, N, W = S5_CHUNK, S5_GROUPS, S5_STATE, S5_GROUP
    J, GP = S5_TILES, S5_GROUPS // S5_TILES
    hp = lax.Precision.HIGHEST
    a_re = jnp.minimum(A_re.astype(F32), -1e-4)
    a_im = A_im.astype(F32)
    dt = jnp.exp(log_step.astype(F32))[:, None]
    lam_re, lam_im = dt * a_re, dt * a_im

    def apow(m):
        m = jnp.asarray(m, F32).reshape(-1, 1, 1)
        mag = jnp.exp(m * lam_re)
        return mag * jnp.cos(m * lam_im), mag * jnp.sin(m * lam_im)

    pw_re, pw_im = apow(np.arange(L + 1))
    ab_re, ab_im = pw_re[1], pw_im[1]
    den = a_re * a_re + a_im * a_im
    z_re = ((ab_re - 1.0) * a_re + ab_im * a_im) / den
    z_im = (ab_im * a_re - (ab_re - 1.0) * a_im) / den
    Br, Bi = B_re.astype(F32), B_im.astype(F32)
    bb_re = z_re[..., None] * Br - z_im[..., None] * Bi
    bb_im = z_re[..., None] * Bi + z_im[..., None] * Br
    Cr, Ci = C_re.astype(F32), C_im.astype(F32)
    eye = jnp.eye(GP, dtype=F32)

    ca_re = Cr[None] * pw_re[:L, :, None, :] - Ci[None] * pw_im[:L, :, None, :]
    ca_im = Cr[None] * pw_im[:L, :, None, :] + Ci[None] * pw_re[:L, :, None, :]
    kern = (jnp.einsum('lgcn,gnd->lgcd', ca_re, bb_re, precision=hp)
            - jnp.einsum('lgcn,gnd->lgcd', ca_im, bb_im, precision=hp))
    d_i = np.arange(L // 2)[:, None, None]
    la_i = np.arange(2)[None, :, None]
    lb_i = np.arange(2)[None, None, :]
    lag = 2 * d_i + lb_i - la_i
    tt = jnp.where((lag >= 0)[..., None, None, None], kern[np.clip(lag, 0, L - 1)], 0.0)
    tt = tt.reshape(L // 2, 2, 2, J, GP, W, W).transpose(0, 3, 1, 4, 6, 2, 5)
    tt = tt[:, :, :, :, :, :, None, :] * eye[None, None, None, :, None, None, :, None]
    tt = tt.reshape(L // 2, J, 2 * GP * W, 2 * GP * W)

    s_re = jnp.stack([ab_re[..., None] * bb_re - ab_im[..., None] * bb_im, bb_re])
    s_im = jnp.stack([ab_re[..., None] * bb_im + ab_im[..., None] * bb_re, bb_im])
    sb = jnp.stack([s_re, s_im])
    sb = sb.reshape(2, 2, J, GP, N, W).transpose(2, 1, 3, 5, 0, 4)
    sb = sb[:, :, :, :, :, None, :] * eye[None, None, :, None, None, :, None]
    sb = sb.reshape(J, 2 * GP * W, 2 * GP * N)

    o_re = jnp.stack([Cr, ca_re[1]])
    o_im = jnp.stack([Ci, ca_im[1]])
    cc = jnp.stack([o_re, -o_im])
    cc = cc.reshape(2, 2, J, GP, W, N).transpose(2, 0, 3, 5, 1, 4)
    cc = cc[:, :, :, :, :, None, :] * eye[None, None, :, None, None, :, None]
    cc = cc.reshape(J, 2 * GP * N, 2 * GP * W)

    sp_re, sp_im = apow(L * (2 ** np.arange(nsteps)))
    a12 = jnp.stack([pw_re[1], pw_im[1], pw_re[2], pw_im[2]]).reshape(4, G * N)
    return (tt.astype(BF16), sb.astype(BF16), cc.astype(BF16), a12,
            sp_re.reshape(nsteps, G * N), sp_im.reshape(nsteps, G * N))


def _s5_kernel(u_ref, tt_ref, sb_ref, cc_ref, a12_ref, spre_ref, spim_ref, d_ref, wg_ref, bg_ref, o_ref,
               xp_ref, are_ref, aim_ref, bre_ref, bim_ref, ys_ref, *, K, pad, nsteps):
    L, J = S5_CHUNK, S5_TILES
    NP = L // 2
    TW = S5_WIDTH // J
    SW = S5_GROUPS * S5_STATE // J

    for l in range(L):
        for j in range(J):
            xp_ref[l // 2, j, :, (l % 2) * TW:(l % 2 + 1) * TW] = u_ref[0, :, l * S5_WIDTH + j * TW:l * S5_WIDTH + (j + 1) * TW]

    bufs = ((are_ref, aim_ref), (bre_ref, bim_ref))
    (fre, fim), (dre, dim) = bufs[nsteps % 2], bufs[(nsteps + 1) % 2]
    zeros = jnp.zeros((pad, SW), F32)

    def state_tile(j, carry):
        for r in (are_ref, aim_ref, bre_ref, bim_ref):
            r[j, 0:pad, :] = zeros
        a1r, a1i, a2r, a2i = a12_ref[j, 0:1, :], a12_ref[j, 1:2, :], a12_ref[j, 2:3, :], a12_ref[j, 3:4, :]
        acc_re = acc_im = None
        for a in range(NP):
            bu = _dot(xp_ref[a, j], sb_ref[j])
            if a == 0:
                acc_re, acc_im = bu[:, :SW], bu[:, SW:]
            else:
                acc_re, acc_im = (a2r * acc_re - a2i * acc_im + bu[:, :SW],
                                  a2r * acc_im + a2i * acc_re + bu[:, SW:])
        are_ref[j, pad:pad + K, :] = acc_re
        aim_ref[j, pad:pad + K, :] = acc_im
        for s in range(nsteps):
            d = 1 << s
            (sre, sim), (tre, tim) = bufs[s % 2], bufs[(s + 1) % 2]
            pr_, pi_ = spre_ref[j, s:s + 1, :], spim_ref[j, s:s + 1, :]
            cre, cim = sre[j, pad:pad + K, :], sim[j, pad:pad + K, :]
            hre, him = sre[j, pad - d:pad - d + K, :], sim[j, pad - d:pad - d + K, :]
            tre[j, pad:pad + K, :] = cre + pr_ * hre - pi_ * him
            tim[j, pad:pad + K, :] = cim + pr_ * him + pi_ * hre
        xre, xim = fre[j, pad - 1:pad - 1 + K, :], fim[j, pad - 1:pad - 1 + K, :]
        dre[j, pad:pad + K, :] = a1r * xre - a1i * xim
        dim[j, pad:pad + K, :] = a1r * xim + a1i * xre
        for b in range(NP):
            cre, cim = dre[j, pad:pad + K, :], dim[j, pad:pad + K, :]
            acc = _dot(jnp.concatenate([_bf(cre), _bf(cim)], axis=1), cc_ref[j])
            for a in range(b + 1):
                acc = acc + _dot(xp_ref[a, j], tt_ref[b - a, j])
            ys_ref[b, j] = acc
            if b + 1 < NP:
                dre[j, pad:pad + K, :] = a2r * cre - a2i * cim
                dim[j, pad:pad + K, :] = a2r * cim + a2i * cre
        return carry

    lax.fori_loop(0, J, state_tile, 0)

    def pair(b, carry):
        for lb in range(2):
            y = jnp.concatenate([ys_ref[b, j, :, lb * TW:(lb + 1) * TW] for j in range(J)], axis=1)
            u = jnp.concatenate([xp_ref[b, j, :, lb * TW:(lb + 1) * TW] for j in range(J)], axis=1)
            y = y + d_ref[...] * u.astype(F32)
            z = 0.5 * y * (1.0 + jnp.tanh(math.sqrt(2.0 / math.pi) * (y + 0.044715 * (y * y * y))))
            o_ref[0, b, :, lb * S5_WIDTH:(lb + 1) * S5_WIDTH] = _bf(
                z * _sigmoid(_dot(_bf(z), wg_ref[...]) + bg_ref[...]))
        return carry

    lax.fori_loop(0, NP, pair, 0)


def _s5(ub, tabs, D, w_glu, b_glu):
    B, S, _ = ub.shape
    L, J = S5_CHUNK, S5_TILES
    NP = L // 2
    K = S // L
    tt, sb, cc, a12, sp_re, sp_im = tabs
    nsteps = sp_re.shape[0]
    pad = max(8, K // 2)
    SW = S5_GROUPS * S5_STATE // J
    tile_major = lambda t: t.reshape(t.shape[0], J, SW).transpose(1, 0, 2)
    small = [tt, sb, cc, tile_major(a12), tile_major(sp_re), tile_major(sp_im),
             D.reshape(1, -1), _bf(w_glu), b_glu.reshape(1, -1)]
    out = pl.pallas_call(
        functools.partial(_s5_kernel, K=K, pad=pad, nsteps=nsteps),
        name="s5",
        grid=(B,),
        in_specs=[pl.BlockSpec((1, K, L * S5_WIDTH), lambda b: (b, 0, 0))]
        + [_full(a.shape, single=True) for a in small],
        out_specs=pl.BlockSpec((1, NP, K, 2 * S5_WIDTH), lambda b: (b, 0, 0, 0)),
        out_shape=jax.ShapeDtypeStruct((B, NP, K, 2 * S5_WIDTH), BF16),
        scratch_shapes=[pltpu.VMEM((NP, J, K, 2 * S5_WIDTH // J), BF16)]
        + [pltpu.VMEM((J, pad + K, SW), F32)] * 4
        + [pltpu.VMEM((NP, J, K, 2 * S5_WIDTH // J), F32)],
        compiler_params=_cparams("parallel"),
    )(ub.reshape(B, K, L * S5_WIDTH), *small)
    return out.transpose(0, 2, 1, 3).reshape(B * S, S5_WIDTH)


def _rwkv_tables(C, nsub):
    GH, HD, GW = RW_GROUP_HEADS, RW_HEAD, RW_GROUP_W
    t = np.arange(C)[:, None]
    s = np.arange(GW)[None, :] % C
    lane = np.arange(GW)
    lanemask = np.zeros((8, GW), np.float32)
    for h in range(GH):
        lanemask[h] = (lane // HD) == h
    bd = ((lane[:, None] // HD) == (lane[None, :] // HD)).astype(np.float32)
    c = np.arange(RW_WIDTH)
    ones = ((c[:, None] // HD) == (c[None, :] // HD)).astype(np.float32)
    r = np.arange(nsub * C)
    tri = ((r[:, None] >= r[None, :]) & ((r[:, None] // C) == (r[None, :] // C))).astype(np.float32)
    return (jnp.asarray((s < t), F32), jnp.asarray((s <= t), F32), jnp.asarray((s == t), F32),
            jnp.asarray(bd), jnp.asarray(lanemask, BF16), jnp.asarray(ones, BF16), jnp.asarray(tri, BF16))


def _rwkv_kernel(*refs, C, nsub, has_vres):
    if has_vres:
        (p_ref, vf_ref, mu_ref, w0_ref, wup_ref, a0_ref, aup_ref, gup_ref, kk_ref, ka_ref, rk_ref,
         lnw_ref, lnb_ref, vup_ref, vb_ref,
         strict_ref, incl_ref, eye_ref, bd_ref, lm_ref, ones_ref, tri_ref, o_ref, st_ref, prev_ref) = refs
    else:
        (p_ref, mu_ref, w0_ref, wup_ref, a0_ref, aup_ref, gup_ref, kk_ref, ka_ref, rk_ref,
         lnw_ref, lnb_ref,
         strict_ref, incl_ref, eye_ref, bd_ref, lm_ref, ones_ref, tri_ref, o_ref, vout_ref, st_ref, prev_ref) = refs

    @pl.when(pl.program_id(1) == 0)
    def _():
        st_ref[...] = jnp.zeros_like(st_ref)
        prev_ref[...] = jnp.zeros_like(prev_ref)

    W = RW_WIDTH
    R = nsub * C
    x = p_ref[0]
    rolled = pltpu.roll(x, 1, axis=0)
    row = lax.broadcasted_iota(jnp.int32, x.shape, 0)
    shifted = jnp.where(row == 0, prev_ref[0:1, :], rolled)
    prev_ref[0:1, :] = x[R - 1:R, :]
    m = x + (shifted - x) * mu_ref[...]
    r, k, v = m[:, 0:W], m[:, W:2 * W], m[:, 2 * W:3 * W]
    wd, ad, gd = m[:, 3 * W:3 * W + 128], m[:, 3 * W + 128:3 * W + 256], m[:, 3 * W + 256:3 * W + 384]
    zw = -(w0_ref[...] + _dot(_bf(jnp.tanh(wd)), wup_ref[...]))
    softplus = jnp.maximum(zw, 0.0) + jnp.log(1.0 + jnp.exp(-jnp.abs(zw)))
    logw = -jnp.exp(-softplus - 0.5)
    lr = _sigmoid(a0_ref[...] + _dot(_bf(ad), aup_ref[...]))
    gate = _dot(_bf(_sigmoid(gd)), gup_ref[...])
    if has_vres:
        vl = x[:, 3 * W + 384:3 * W + 512]
        v = v + (vf_ref[0] - v) * _sigmoid(vb_ref[...] + _dot(_bf(vl), vup_ref[...]))
    else:
        vout_ref[0] = v
    ones = ones_ref[...]
    kn = k * kk_ref[...]
    kn = kn * lax.rsqrt(_dot_exact_rhs(kn * kn, ones) + 1e-12)
    k2 = k * (1.0 + (lr - 1.0) * ka_ref[...])
    cum = _dot_exact_lhs(tri_ref[...], logw)
    p_inv = jnp.exp(-cum)
    a_all = -kn * jnp.exp(cum - logw)
    r_all = r * jnp.exp(cum)
    b_all = kn * lr * p_inv
    k_all = k2 * p_inv
    knlr = kn * lr

    strict, incl, eye, bd = strict_ref[...], incl_ref[...], eye_ref[...], bd_ref[...]
    GW = RW_GROUP_W
    NG = RW_HEADS // RW_GROUP_HEADS
    HC = RW_GROUP_HEADS * C

    def stack(z):
        zb = _bf(z)
        return jnp.concatenate([zb * lm_ref[h:h + 1, :] for h in range(RW_GROUP_HEADS)], axis=0)

    chains = [(sc, gi) for sc in range(nsub) for gi in range(NG)]
    rsl = lambda sc: slice(sc * C, (sc + 1) * C)
    lsl = lambda gi: slice(gi * GW, (gi + 1) * GW)
    ar, vs, n_w, aak, arbk = {}, {}, {}, {}, {}
    for c in chains:
        rs, sl = rsl(c[0]), lsl(c[1])
        ar[c] = _bf(jnp.concatenate([a_all[rs, sl], r_all[rs, sl]], axis=0))
        vs[c] = stack(v[rs, sl])
        ybk = jnp.concatenate([stack(b_all[rs, sl]), stack(k_all[rs, sl])], axis=0)
        p1 = _dot_nt(ar[c], ybk)
        n_w[c] = p1[0:C, 0:HC] * strict
        aak[c] = _bf(p1[0:C, HC:2 * HC] * strict)
        arbk[c] = _bf(jnp.concatenate([p1[C:2 * C, 0:HC] * incl, p1[C:2 * C, HC:2 * HC] * incl], axis=1))
    tinv = {c: eye + n_w[c] for c in chains}
    pw = dict(n_w)
    for _ in range(int(math.log2(C)) - 1):
        pw = {c: _dot(_bf(pw[c]), stack(pw[c])) for c in chains}
        tinv = {c: tinv[c] + _dot(_bf(tinv[c]), stack(pw[c])) for c in chains}
    zv = {c: _dot(aak[c], vs[c]) for c in chains}
    tinv = {c: _bf(tinv[c]) for c in chains}

    states = [st_ref[gi] for gi in range(NG)]
    out_rows = []
    for sc in range(nsub):
        rs = rsl(sc)
        cum_last = cum[(sc + 1) * C - 1:(sc + 1) * C, :]
        p_end = jnp.exp(cum_last - cum[rs])
        p_last = jnp.exp(cum_last)
        bke = {gi: _bf(jnp.concatenate([knlr[rs, lsl(gi)] * p_end[:, lsl(gi)], k2[rs, lsl(gi)] * p_end[:, lsl(gi)]], axis=0))
               for gi in range(NG)}
        zr = {gi: _dot_nt(ar[(sc, gi)], _bf(states[gi])) for gi in range(NG)}
        u = {gi: _dot(tinv[(sc, gi)], stack(zr[gi][0:C] + zv[(sc, gi)])) for gi in range(NG)}
        outs = [zr[gi][C:2 * C] + _dot(arbk[(sc, gi)], jnp.concatenate([stack(u[gi]), vs[(sc, gi)]], axis=0))
                for gi in range(NG)]
        for gi in range(NG):
            upd = _dot_tn(_bf(jnp.concatenate([u[gi], v[rs, lsl(gi)]], axis=0)), bke[gi])
            states[gi] = states[gi] * p_last[:, lsl(gi)] + upd * bd
        out_rows.append(jnp.concatenate(outs, axis=1))
    for gi in range(NG):
        st_ref[gi] = states[gi]
    y = jnp.concatenate(out_rows, axis=0)
    inv_n = 1.0 / RW_HEAD
    mean = _dot_exact_rhs(y, ones) * inv_n
    yc = y - mean
    var = _dot_exact_rhs(yc * yc, ones) * inv_n
    yn = yc * lax.rsqrt(var + RW_GN_EPS) * lnw_ref[...] + lnb_ref[...]
    bonus = _dot_exact_rhs(r * k2 * rk_ref[...], ones) * v
    o_ref[0] = _bf((yn + bonus) * gate)


def _rwkv(p3, v_first, prm, *, C=RW_CHUNK, nsub=RW_SUBCHUNKS):
    B, S, _ = p3.shape
    C = min(C, S)
    R = nsub * C
    assert S % R == 0
    has_vres = v_first is not None
    tabs = _rwkv_tables(C, nsub)
    W = RW_WIDTH
    NG = RW_HEADS // RW_GROUP_HEADS
    row = lambda a: a.reshape(1, -1)
    args = [p3]
    specs = [pl.BlockSpec((1, R, 2048), lambda b, i: (b, i, P_RW // 2048))]
    if has_vres:
        args.append(v_first)
        specs.append(pl.BlockSpec((1, R, W), lambda b, i: (b, i, 0)))
    small = [row(prm["mu"]), row(prm["w0"]), prm["w_up"], row(prm["a0"]), prm["a_up"], prm["g_up"],
             row(prm["k_k"]), row(prm["k_a"]), row(prm["r_k"]), row(prm["ln_w"]), row(prm["ln_b"])]
    if has_vres:
        small += [prm["v_up"], row(prm["v_bias"])]
    small += list(tabs)
    args += small
    specs += [_full(a.shape) for a in small]
    o_spec = pl.BlockSpec((1, R, W), lambda b, i: (b, i, 0))
    if has_vres:
        out_specs, out_shape = o_spec, jax.ShapeDtypeStruct((B, S, W), BF16)
    else:
        out_specs = [o_spec, o_spec]
        out_shape = [jax.ShapeDtypeStruct((B, S, W), BF16), jax.ShapeDtypeStruct((B, S, W), F32)]
    res = pl.pallas_call(
        functools.partial(_rwkv_kernel, C=C, nsub=nsub, has_vres=has_vres),
        name="rwkv7",
        grid=(B, S // R),
        in_specs=specs, out_specs=out_specs, out_shape=out_shape,
        scratch_shapes=[pltpu.VMEM((NG, RW_GROUP_W, RW_GROUP_W), F32), pltpu.VMEM((8, 2048), F32)],
        compiler_params=_cparams("parallel", "arbitrary"),
    )(*args)
    return (res, v_first) if has_vres else (res[0], res[1])


def _merge_kernel(x_ref, o0_ref, o1_ref, o2_ref, o3_ref, g_ref, wg_ref, wb_ref, wo_ref, y_ref):
    x = x_ref[...]
    D = x.shape[1]
    hn = _bf(_rms(x, g_ref[...]))
    acc = None
    for m, o_ref in enumerate((o0_ref, o1_ref, o2_ref, o3_ref)):
        gate = _sigmoid(_dot(hn, wg_ref[:, m * D:(m + 1) * D]))
        t = gate * _dot(o_ref[...], wb_ref[m])
        acc = t if acc is None else acc + t
    y_ref[...] = x + _dot(_bf(acc), wo_ref[...])


def _merge(x, g, outs, wg, wb, wo, *, tm=256):
    T, D = x.shape
    tm = min(tm, T)
    bw = outs[0].shape[1]
    return pl.pallas_call(
        _merge_kernel,
        name="gated_merge",
        grid=(T // tm,),
        in_specs=[pl.BlockSpec((tm, D), lambda i: (i, 0))]
        + [pl.BlockSpec((tm, bw), lambda i: (i, 0))] * 4
        + [_full((1, D)), _full(wg.shape), _full(wb.shape), _full(wo.shape)],
        out_specs=pl.BlockSpec((tm, D), lambda i: (i, 0)),
        out_shape=jax.ShapeDtypeStruct((T, D), F32),
        compiler_params=_cparams("parallel"),
    )(x, *outs, g.reshape(1, D), wg, wb, wo)


def _xattn_kernel(x_ref, kv_ref, g_ref, wq_ref, wo_ref, y_ref):
    x = x_ref[0]
    D = x.shape[1]
    hd = D // X_HEADS
    q = _bf(_dot(_bf(_rms(x, g_ref[...])), wq_ref[...]) * (hd ** -0.5))
    outs = []
    for h in range(X_HEADS):
        kh = kv_ref[0, :, h * hd:(h + 1) * hd]
        vh = kv_ref[0, :, D + h * hd:D + (h + 1) * hd]
        s = _dot_nt(q[:, h * hd:(h + 1) * hd], kh)
        e = jnp.exp(s - jnp.max(s, axis=-1, keepdims=True))
        pr = e / jnp.sum(e, axis=-1, keepdims=True)
        outs.append(_bf(_dot(_bf(pr), vh)))
    y_ref[0] = x + _dot(jnp.concatenate(outs, axis=1), wo_ref[...])


def _xattn(x3, kv3, g, wq, wo, *, tm=512):
    B, S, D = x3.shape
    M = kv3.shape[1]
    tm = min(tm, S)
    return pl.pallas_call(
        _xattn_kernel,
        name="cross_attn",
        grid=(B, S // tm),
        in_specs=[pl.BlockSpec((1, tm, D), lambda b, i: (b, i, 0)),
                  pl.BlockSpec((1, M, 2 * D), lambda b, i: (b, 0, 0)),
                  _full((1, D)), _full(wq.shape), _full(wo.shape)],
        out_specs=pl.BlockSpec((1, tm, D), lambda b, i: (b, i, 0)),
        out_shape=jax.ShapeDtypeStruct((B, S, D), F32),
        compiler_params=_cparams("parallel", "parallel"),
    )(x3, kv3, g.reshape(1, D), wq, wo)


HALO = 16


def _ffn_kernel(x_ref, halo_ref, g_ref, wg_ref, wu_ref, cw_ref, cb_ref, wd_ref, gf_ref, y_ref,
                hn_ref, gbuf_ref, acc_ref, *, tm, final_norm):
    i = pl.program_id(1)
    f = pl.program_id(2)

    @pl.when(f == 0)
    def _():
        hn_ref[0:HALO, :] = _bf(_rms(halo_ref[0], g_ref[...]))
        hn_ref[HALO:HALO + tm, :] = _bf(_rms(x_ref[0], g_ref[...]))
        acc_ref[...] = jnp.zeros_like(acc_ref)

    hn = hn_ref[...]
    gpre = _dot(hn, wg_ref[...])
    keep = (i > 0).astype(F32)
    gbuf_ref[0:HALO, :] = gpre[0:HALO] * keep
    gbuf_ref[HALO:HALO + tm, :] = gpre[HALO:HALO + tm]
    conv = (cw_ref[0:1, :] * gbuf_ref[HALO - 2:HALO - 2 + tm, :]
            + cw_ref[1:2, :] * gbuf_ref[HALO - 1:HALO - 1 + tm, :]
            + cw_ref[2:3, :] * gbuf_ref[HALO:HALO + tm, :] + cb_ref[...])
    up = _dot(hn[HALO:HALO + tm], wu_ref[...])
    act = conv * _sigmoid(conv) * up
    acc_ref[...] += _dot(_bf(act), wd_ref[...])

    @pl.when(f == pl.num_programs(2) - 1)
    def _():
        y = x_ref[0] + acc_ref[...]
        if final_norm:
            y = _rms(y, gf_ref[...])
        y_ref[0] = y


def _ffn(x3, g, wg, wu, cw, cb, wd, gf, *, final_norm, tm=512, tf=1408):
    B, S, D = x3.shape
    F = wg.shape[1]
    tm = min(tm, S)
    hb = tm // HALO
    return pl.pallas_call(
        functools.partial(_ffn_kernel, tm=tm, final_norm=final_norm),
        name="conv_ffn",
        grid=(B, S // tm, F // tf),
        in_specs=[pl.BlockSpec((1, tm, D), lambda b, i, f: (b, i, 0)),
                  pl.BlockSpec((1, HALO, D), lambda b, i, f: (b, jnp.maximum(i * hb - 1, 0), 0)),
                  pl.BlockSpec((1, D), lambda b, i, f: (0, 0)),
                  pl.BlockSpec((D, tf), lambda b, i, f: (0, f)),
                  pl.BlockSpec((D, tf), lambda b, i, f: (0, f)),
                  pl.BlockSpec((3, tf), lambda b, i, f: (0, f)),
                  pl.BlockSpec((1, tf), lambda b, i, f: (0, f)),
                  pl.BlockSpec((tf, D), lambda b, i, f: (f, 0)),
                  pl.BlockSpec((1, D), lambda b, i, f: (0, 0))],
        out_specs=pl.BlockSpec((1, tm, D), lambda b, i, f: (b, i, 0)),
        out_shape=jax.ShapeDtypeStruct((B, S, D), F32),
        scratch_shapes=[pltpu.VMEM((HALO + tm, D), BF16), pltpu.VMEM((HALO + tm, tf), F32),
                        pltpu.VMEM((tm, D), F32)],
        compiler_params=_cparams("parallel", "parallel", "arbitrary"),
    )(x3, x3, g.reshape(1, D), wg, wu, cw, cb.reshape(1, F), wd, gf.reshape(1, D))


def _pad_cols(w, n):
    return jnp.pad(w, ((0, 0), (0, n - w.shape[1])))


def _pad_rows(w, n):
    return jnp.pad(w, ((0, n - w.shape[0]), (0, 0)))


def _swap_halves(w):
    h = w.shape[-1] // 2
    return jnp.concatenate([w[..., h:], w[..., :h]], axis=-1)


def _pack_w1(w_in, v_down):
    D = w_in.shape[0]
    w_in = _bf(w_in)
    if v_down is not None:
        v_down = _bf(v_down)
    o = 0
    cq = w_in[:, o:o + 256]; o += 256
    ckv = w_in[:, o:o + 128]; o += 128
    kr = w_in[:, o:o + 64]; o += 64
    hg = w_in[:, o:o + 2048]; o += 2048
    su = w_in[:, o:o + 512]; o += 512
    rw = w_in[:, o:o + 1792]; o += 1792
    gates = w_in[:, o:o + 4096]
    rw_p = jnp.concatenate([rw[:, 0:1536], _pad_cols(rw[:, 1536:1600], 128), _pad_cols(rw[:, 1600:1664], 128),
                            rw[:, 1664:1792],
                            _pad_cols(v_down, 128) if v_down is not None else jnp.zeros((D, 128), w_in.dtype)],
                           axis=1)
    mla = jnp.concatenate([cq, ckv, kr, _swap_halves(kr)], axis=1)
    w1 = jnp.concatenate([hg, rw_p, mla, su], axis=1)
    assert w1.shape[1] == P_WIDTH
    return w1, gates


def _pack_mla(w_uq, w_ukv):
    H = MLA_HEADS
    wq = w_uq.reshape(MLA_Q_RANK, H, MLA_QK)
    nope, pe = wq[..., :MLA_NOPE], wq[..., MLA_NOPE:]
    z = jnp.zeros((MLA_Q_RANK, H, 64), w_uq.dtype)
    wq_p = jnp.concatenate([nope, pe, z, _swap_halves(pe), z], axis=-1).reshape(MLA_Q_RANK, H * 384)
    return _bf(wq_p), _bf(w_ukv)


def kernel(x, mem, positions, norm_mix, w_in, mla_q_norm, mla_w_uq, mla_kv_norm, mla_w_ukv, hgrn_lb_logits, hgrn_o_norm, s5_A_re, s5_A_im, s5_log_step, s5_B_re, s5_B_im, s5_C_re, s5_C_im, s5_D, s5_w_glu, s5_b_glu, rwkv_mu, rwkv_w0, rwkv_w_up, rwkv_a0, rwkv_a_up, rwkv_g_up, rwkv_k_k, rwkv_k_a, rwkv_r_k, rwkv_ln_w, rwkv_ln_b, rwkv_vres_down, rwkv_vres_up, rwkv_vres_bias, w_branch_mla, w_branch_hgrn, w_branch_s5, w_branch_rwkv, w_out, norm_xq, norm_xm, xattn_w_q, xattn_w_kv, xattn_w_o, norm_ffn, ffn_w_gate_up, ffn_conv_w, ffn_conv_b, ffn_w_down, norm_final):
    B, S, D = x.shape
    T = B * S
    depth = norm_mix.shape[0]
    M = mem.shape[1]

    lb_p = jax.nn.softmax(hgrn_lb_logits.astype(F32), axis=0)
    lb_c = jnp.cumsum(lb_p, axis=0)
    lower_bounds = lb_c - lb_c[0:1]

    half = MLA_ROPE // 2
    inv_freq = ROPE_THETA ** (-np.arange(half, dtype=np.float32) / half)
    frq = jnp.asarray(np.concatenate([inv_freq, inv_freq])[None, :], F32)
    sgn = jnp.asarray(np.concatenate([-np.ones(half), np.ones(half)])[None, :], F32)
    pos3 = positions.astype(F32).reshape(B, S, 1)

    nsteps = int(math.log2(S // S5_CHUNK))
    mem2 = mem.reshape(B * M, D)
    x2 = x.reshape(T, D)
    v_first = None
    for l in range(depth):
        w1, wgates = _pack_w1(w_in[l], rwkv_vres_down[l - 1] if l > 0 else None)
        p, ub = _in_proj(x2, norm_mix[l], w1)
        p3 = p.reshape(B, S, P_S5)

        wq_p, wkv_p = _pack_mla(mla_w_uq[l], mla_w_ukv[l])
        q, k, v = _mla_prep(p3, pos3, frq, sgn, mla_q_norm[l].reshape(1, -1), wq_p,
                            mla_kv_norm[l].reshape(1, -1), wkv_p)
        o_mla = _mla_attn(q, k, v).reshape(T, -1)

        o_hg = _hgrn(p3, lower_bounds[l], hgrn_o_norm[l]).reshape(T, -1)

        s5_tabs = _s5_tables(s5_A_re[l], s5_A_im[l], s5_log_step[l], s5_B_re[l], s5_B_im[l],
                             s5_C_re[l], s5_C_im[l], nsteps)
        o_s5 = _s5(ub.reshape(B, S, S5_WIDTH), s5_tabs, s5_D[l], s5_w_glu[l], s5_b_glu[l])

        mu = rwkv_mu[l]
        mu_p = jnp.concatenate([mu[0:1536], jnp.pad(mu[1536:1600], (0, 64)), jnp.pad(mu[1600:1664], (0, 64)),
                                mu[1664:1792], jnp.zeros((128,), mu.dtype)])
        prm = dict(mu=mu_p, w0=rwkv_w0[l], w_up=_bf(_pad_rows(rwkv_w_up[l], 128)), a0=rwkv_a0[l],
                   a_up=_bf(_pad_rows(rwkv_a_up[l], 128)), g_up=_bf(rwkv_g_up[l]), k_k=rwkv_k_k[l],
                   k_a=rwkv_k_a[l], r_k=rwkv_r_k[l], ln_w=rwkv_ln_w[l], ln_b=rwkv_ln_b[l])
        if l > 0:
            prm["v_up"] = _bf(_pad_rows(rwkv_vres_up[l - 1], 128))
            prm["v_bias"] = rwkv_vres_bias[l - 1]
        o_rw, v_first = _rwkv(p3, v_first, prm)
        o_rw = o_rw.reshape(T, -1)

        wb = _bf(jnp.stack([w_branch_mla[l], w_branch_hgrn[l], w_branch_s5[l], w_branch_rwkv[l]]))
        x2 = _merge(x2, norm_mix[l], (o_mla, o_hg, o_s5, o_rw), wgates, wb, _bf(w_out[l]))

        kv = _norm_proj(mem2, norm_xm[l], _bf(xattn_w_kv[l]), out_dtype=BF16, name="xattn_kv_proj")
        x3 = _xattn(x2.reshape(B, S, D), kv.reshape(B, M, 2 * D), norm_xq[l], _bf(xattn_w_q[l]),
                    _bf(xattn_w_o[l]))

        wgu = ffn_w_gate_up[l]
        x3 = _ffn(x3, norm_ffn[l], _bf(wgu[:, :D_FF]), _bf(wgu[:, D_FF:]), ffn_conv_w[l], ffn_conv_b[l],
                  _bf(ffn_w_down[l]), norm_final, final_norm=(l == depth - 1))
        x2 = x3.reshape(T, D)
    return x2.reshape(B, S, D)
```

```python
import functools
import math

import numpy as np
import jax
import jax.numpy as jnp
from jax import lax
from jax.experimental import pallas as pl
from jax.experimental.pallas import tpu as pltpu

F32 = jnp.float32
BF16 = jnp.bfloat16

V7X_VMEM_LIMIT_BYTES = 56 * 1024 * 1024
LANES = 128

EPS = 1e-6
MLA_HEADS = 4
MLA_Q_RANK = 256
MLA_KV_RANK = 128
MLA_NOPE = 128
MLA_ROPE = 64
MLA_V = 128
MLA_QK = MLA_NOPE + MLA_ROPE
ROPE_THETA = 10000.0
HG_HEADS = 4
HG_DK = 128
HG_W = HG_HEADS * HG_DK
S5_WIDTH = 512
S5_GROUP = 16
S5_GROUPS = S5_WIDTH // S5_GROUP
S5_STATE = 64
S5_CHUNK = 16
S5_TILES = 4
RW_HEADS = 8
RW_HEAD = 64
RW_WIDTH = RW_HEADS * RW_HEAD
RW_GN_EPS = 64e-5
RW_MV_LORA = 32
X_HEADS = 4
D_FF = 2816

P_HG = 0
P_RW = 2048
P_MLA = 4096
P_S5 = 4608
P_WIDTH = 5120

HG_CHUNK = 128
HG_SEL_BELOW = 8
RW_CHUNK = 64
RW_SUBCHUNKS = 4
RW_GROUP_HEADS = 4
RW_GROUP_W = RW_GROUP_HEADS * RW_HEAD


def _cparams(*sem):
    return pltpu.CompilerParams(dimension_semantics=sem, vmem_limit_bytes=V7X_VMEM_LIMIT_BYTES)


def _dot(a, b):
    return jnp.dot(a, b, preferred_element_type=F32)


def _dot_nt(a, b):
    return lax.dot_general(a, b, (((1,), (1,)), ((), ())), preferred_element_type=F32)


def _dot_tn(a, b):
    return lax.dot_general(a, b, (((0,), (0,)), ((), ())), preferred_element_type=F32)


def _bf(x):
    return x.astype(BF16)


def _split2(x):
    hi = x.astype(BF16)
    lo = (x - hi.astype(F32)).astype(BF16)
    return hi, lo


def _split3(x):
    hi = x.astype(BF16)
    r = x - hi.astype(F32)
    mid = r.astype(BF16)
    lo = (r - mid.astype(F32)).astype(BF16)
    return hi, mid, lo


def _dot_exact_lhs(m_bf16, x):
    x0, x1, x2 = _split3(x)
    return _dot(m_bf16, x0) + _dot(m_bf16, x1) + _dot(m_bf16, x2)


def _dot_exact_rhs(x, m_bf16):
    x0, x1 = _split2(x)
    return _dot(x0, m_bf16) + _dot(x1, m_bf16)


def _rms(x, g):
    return x * lax.rsqrt(jnp.mean(x * x, axis=-1, keepdims=True) + EPS) * g


def _sigmoid(x):
    return 1.0 / (1.0 + jnp.exp(-x))


def _full(shape, single=False):
    n = len(shape)
    if single:
        return pl.BlockSpec(shape, lambda *_: (0,) * n, pipeline_mode=pl.Buffered(1))
    return pl.BlockSpec(shape, lambda *_: (0,) * n)


def _in_proj_kernel(x_ref, g_ref, w_ref, p_ref, u_ref):
    y = _dot(_bf(_rms(x_ref[...], g_ref[...])), w_ref[...])
    p_ref[...] = y[:, 0:P_S5]
    u_ref[...] = _bf(y[:, P_S5:P_WIDTH])


def _in_proj(x, g, w, *, tm=256):
    M, K = x.shape
    tm = min(tm, M)
    assert M % tm == 0 and w.shape[1] == P_WIDTH
    return pl.pallas_call(
        _in_proj_kernel,
        name="in_proj",
        grid=(M // tm,),
        in_specs=[pl.BlockSpec((tm, K), lambda i: (i, 0)), _full((1, K)), _full((K, P_WIDTH))],
        out_specs=[pl.BlockSpec((tm, P_S5), lambda i: (i, 0)), pl.BlockSpec((tm, S5_WIDTH), lambda i: (i, 0))],
        out_shape=[jax.ShapeDtypeStruct((M, P_S5), F32), jax.ShapeDtypeStruct((M, S5_WIDTH), BF16)],
        compiler_params=_cparams("parallel"),
    )(x, g.reshape(1, K), w)


def _norm_proj_kernel(x_ref, g_ref, w_ref, o_ref):
    o_ref[...] = _dot(_bf(_rms(x_ref[...], g_ref[...])), w_ref[...]).astype(o_ref.dtype)


def _norm_proj(x, g, w, *, out_dtype=F32, tm=256, name="norm_proj"):
    M, K = x.shape
    N = w.shape[1]
    tm = min(tm, M)
    assert M % tm == 0
    return pl.pallas_call(
        _norm_proj_kernel,
        name=name,
        grid=(M // tm,),
        in_specs=[pl.BlockSpec((tm, K), lambda i: (i, 0)), _full((1, K)), _full((K, N))],
        out_specs=pl.BlockSpec((tm, N), lambda i: (i, 0)),
        out_shape=jax.ShapeDtypeStruct((M, N), out_dtype),
        compiler_params=_cparams("parallel"),
    )(x, g.reshape(1, K), w)


def _mla_prep_kernel(p_ref, pos_ref, frq_ref, sgn_ref, qn_ref, wq_ref, kvn_ref, wkv_ref,
                     q_ref, k_ref, v_ref):
    p = p_ref[0]
    cq = p[:, 0:MLA_Q_RANK]
    ckv = p[:, MLA_Q_RANK:MLA_Q_RANK + MLA_KV_RANK]
    kr = p[:, 384:448]
    kr_sw = p[:, 448:512]
    ang = pos_ref[0] * frq_ref[...]
    cc = jnp.cos(ang)
    ss = jnp.sin(ang) * sgn_ref[...]
    qa = _dot(_bf(_rms(cq, qn_ref[...])), wq_ref[...])
    kva = _dot(_bf(_rms(ckv, kvn_ref[...])), wkv_ref[...])
    kpe = _bf(kr * cc + kr_sw * ss)
    scale = MLA_QK ** -0.5
    for h in range(MLA_HEADS):
        qb = h * 384
        q_pe = qa[:, qb + 128:qb + 192] * cc + qa[:, qb + 256:qb + 320] * ss
        q_ref[0, h, :, 0:MLA_NOPE] = _bf(qa[:, qb:qb + 128] * scale)
        q_ref[0, h, :, MLA_NOPE:MLA_QK] = _bf(q_pe * scale)
        kb = h * 256
        k_ref[0, h, :, 0:MLA_NOPE] = _bf(kva[:, kb:kb + 128])
        k_ref[0, h, :, MLA_NOPE:MLA_QK] = kpe
        v_ref[0, h] = _bf(kva[:, kb + 128:kb + 256])


def _mla_prep(p3, pos3, frq, sgn, qn, wq, kvn, wkv, *, tm=512):
    B, S, _ = p3.shape
    tm = min(tm, S)
    H = MLA_HEADS
    return pl.pallas_call(
        _mla_prep_kernel,
        name="mla_prep",
        grid=(B, S // tm),
        in_specs=[pl.BlockSpec((1, tm, 512), lambda b, i: (b, i, P_MLA // 512)),
                  pl.BlockSpec((1, tm, 1), lambda b, i: (b, i, 0)),
                  _full((1, MLA_ROPE)), _full((1, MLA_ROPE)),
                  _full((1, MLA_Q_RANK)), _full(wq.shape),
                  _full((1, MLA_KV_RANK)), _full(wkv.shape)],
        out_specs=[pl.BlockSpec((1, H, tm, MLA_QK), lambda b, i: (b, 0, i, 0)),
                   pl.BlockSpec((1, H, tm, MLA_QK), lambda b, i: (b, 0, i, 0)),
                   pl.BlockSpec((1, H, tm, MLA_V), lambda b, i: (b, 0, i, 0))],
        out_shape=[jax.ShapeDtypeStruct((B, H, S, MLA_QK), BF16),
                   jax.ShapeDtypeStruct((B, H, S, MLA_QK), BF16),
                   jax.ShapeDtypeStruct((B, H, S, MLA_V), BF16)],
        compiler_params=_cparams("parallel", "parallel"),
    )(p3, pos3, frq, sgn, qn, wq, kvn, wkv)


def _mla_attn_kernel(q_ref, k_ref, v_ref, o_ref, *, tq, hb):
    i = pl.program_id(2)
    qs = [q_ref[0, h] for h in range(hb)]

    def step(h, k, v, carry, mask):
        m, l, acc = carry
        s = _dot_nt(qs[h], k)
        if mask is not None:
            s = jnp.where(mask, s, -1e30)
        m_new = jnp.maximum(m, jnp.max(s, axis=-1, keepdims=True))
        alpha = jnp.exp(m - m_new)
        pr = jnp.exp(s - m_new)
        l = alpha * l + jnp.sum(pr, axis=-1, keepdims=True)
        acc = alpha * acc + _dot(_bf(pr), v)
        return m_new, l, acc

    def blocks(off, carries, mask):
        return tuple(step(h, k_ref[0, h, pl.ds(off, tq), :], v_ref[0, h, pl.ds(off, tq), :], carries[h], mask)
                     for h in range(hb))

    init = tuple((jnp.full((tq, 1), -1e30, F32), jnp.zeros((tq, 1), F32), jnp.zeros((tq, MLA_V), F32))
                 for _ in range(hb))
    carries = lax.fori_loop(0, i, lambda j, c: blocks(pl.multiple_of(j * tq, tq), c, None), init)
    row = lax.broadcasted_iota(jnp.int32, (tq, tq), 0)
    col = lax.broadcasted_iota(jnp.int32, (tq, tq), 1)
    carries = blocks(pl.multiple_of(i * tq, tq), carries, col <= row)
    for h in range(hb):
        _, l, acc = carries[h]
        o_ref[0, :, h * MLA_V:(h + 1) * MLA_V] = _bf(acc / l)


def _mla_attn(q, k, v, *, tq=1024, hb=2):
    B, H, S, _ = q.shape
    tq = min(tq, S)
    return pl.pallas_call(
        functools.partial(_mla_attn_kernel, tq=tq, hb=hb),
        name="mla_attn",
        grid=(B, H // hb, S // tq),
        in_specs=[pl.BlockSpec((1, hb, tq, MLA_QK), lambda b, h, i: (b, h, i, 0)),
                  pl.BlockSpec((1, hb, S, MLA_QK), lambda b, h, i: (b, h, 0, 0)),
                  pl.BlockSpec((1, hb, S, MLA_V), lambda b, h, i: (b, h, 0, 0))],
        out_specs=pl.BlockSpec((1, tq, hb * MLA_V), lambda b, h, i: (b, i, h)),
        out_shape=jax.ShapeDtypeStruct((B, S, H * MLA_V), BF16),
        compiler_params=_cparams("parallel", "parallel", "parallel"),
    )(q, k, v)


def _hgrn_tables(C):
    t = np.arange(C)
    nlev = int(math.log2(C))
    masks, sels, sgns = [], [], []
    for lev in range(nlev):
        n = C >> (lev + 1)
        blk2 = t // (2 * n)
        upper = (t // n) % 2
        masks.append((blk2[:, None] == blk2[None, :]) & (upper[:, None] == 1) & (upper[None, :] == 0))
        if n < HG_SEL_BELOW:
            sel = np.zeros((C, C), np.float32)
            sel[t, blk2 * 2 * n + n - 1] = 1.0
            sels.append(sel)
        sgns.append(np.broadcast_to(np.where(upper == 1, 1.0, -1.0)[:, None], (C, HG_DK)))
    masks.append(np.eye(C, dtype=bool))
    tri = (t[:, None] >= t[None, :]).astype(np.float32)
    return (jnp.asarray(np.stack(masks), F32), jnp.asarray(np.concatenate(sels, 0), BF16),
            jnp.asarray(np.stack(sgns), F32), jnp.asarray(tri, BF16))


def _hgrn_kernel(p_ref, lb_ref, onorm_ref, mask_ref, sel_ref, sgn_ref, tri_ref, o_ref, st_ref, *, C):
    @pl.when(pl.program_id(1) == 0)
    def _():
        st_ref[...] = jnp.zeros_like(st_ref)

    nlev = mask_ref.shape[0] - 1
    x = p_ref[0]
    q = x[:, 0:HG_W]
    fc = x[:, HG_W:2 * HG_W]
    v = x[:, 2 * HG_W:3 * HG_W]
    g = x[:, 3 * HG_W:4 * HG_W]
    lb = lb_ref[...]
    f = lb + (1.0 - lb) * _sigmoid(fc)
    logf = jnp.log(f)
    kk = 1.0 - f
    b = _dot_exact_lhs(tri_ref[...], logf)
    b0, b1 = _split2(b)
    bsel = _dot(sel_ref[...], b0) + _dot(sel_ref[...], b1)
    bmids, nsel = [], 0
    for lev in range(nlev):
        n = C >> (lev + 1)
        if n >= HG_SEL_BELOW:
            parts = [jnp.broadcast_to(b[j * 2 * n + n - 1:j * 2 * n + n, :], (2 * n, HG_W)) for j in range(C // (2 * n))]
            bmids.append(parts[0] if len(parts) == 1 else jnp.concatenate(parts, axis=0))
        else:
            bmids.append(bsel[nsel * C:(nsel + 1) * C, :])
            nsel += 1
    b_last = b[C - 1:C, :]
    qdec = q * jnp.exp(b)
    kdec = kk * jnp.exp(b_last - b)
    hs = range(HG_HEADS)
    sls = [slice(h * HG_DK, (h + 1) * HG_DK) for h in hs]
    att = [_dot_nt(_bf(q[:, sl]), _bf(kk[:, sl])) * mask_ref[nlev] for sl in sls]
    for lev in range(nlev):
        es = [jnp.exp(sgn_ref[lev] * (b[:, sl] - bmids[lev][:, sl])) for sl in sls]
        att = [att[h] + _dot_nt(_bf(q[:, sls[h]] * es[h]), _bf(kk[:, sls[h]] * es[h])) * mask_ref[lev] for h in hs]
    sts = [st_ref[h] for h in hs]
    vhs = [_bf(v[:, sl]) for sl in sls]
    os_ = [_dot(_bf(att[h]), vhs[h]) + _dot_nt(_bf(qdec[:, sls[h]]), _bf(sts[h])) for h in hs]
    for h in hs:
        sl = sls[h]
        st_ref[h] = sts[h] * jnp.exp(b_last[:, sl]) + _dot_tn(vhs[h], _bf(kdec[:, sl]))
        o = os_[h]
        o = o * lax.rsqrt(jnp.mean(o * o, axis=-1, keepdims=True) + EPS) * onorm_ref[:, sl]
        o_ref[0, :, sl] = _bf(o * _sigmoid(g[:, sl]))


def _hgrn(p3, lb, onorm, *, C=HG_CHUNK):
    B, S, _ = p3.shape
    C = min(C, S)
    masks, sel, sgn, tri = _hgrn_tables(C)
    return pl.pallas_call(
        functools.partial(_hgrn_kernel, C=C),
        name="hgrn2",
        grid=(B, S // C),
        in_specs=[pl.BlockSpec((1, C, 4 * HG_W), lambda b, i: (b, i, P_HG // (4 * HG_W))),
                  _full((1, HG_W)), _full((1, HG_W)),
                  _full(masks.shape), _full(sel.shape), _full(sgn.shape), _full(tri.shape)],
        out_specs=pl.BlockSpec((1, C, HG_W), lambda b, i: (b, i, 0)),
        out_shape=jax.ShapeDtypeStruct((B, S, HG_W), BF16),
        scratch_shapes=[pltpu.VMEM((HG_HEADS, HG_DK, HG_DK), F32)],
        compiler_params=_cparams("parallel", "arbitrary"),
    )(p3, lb.reshape(1, HG_W), onorm.reshape(1, HG_W), masks, sel, sgn, tri)


def _s5_tables(A_re, A_im, log_step, B_re, B_im, C_re, C_im, nsteps):
    L, G, N, W = S5_CHUNK, S5_GROUPS, S5_STATE, S5_GROUP
    J, GP = S5_TILES, S5_GROUPS // S5_TILES
    hp = lax.Precision.HIGHEST
    a_re = jnp.minimum(A_re.astype(F32), -1e-4)
    a_im = A_im.astype(F32)
    dt = jnp.exp(log_step.astype(F32))[:, None]
    lam_re, lam_im = dt * a_re, dt * a_im

    def apow(m):
        m = jnp.asarray(m, F32).reshape(-1, 1, 1)
        mag = jnp.exp(m * lam_re)
        return mag * jnp.cos(m * lam_im), mag * jnp.sin(m * lam_im)

    pw_re, pw_im = apow(np.arange(L + 1))
    ab_re, ab_im = pw_re[1], pw_im[1]
    den = a_re * a_re + a_im * a_im
    z_re = ((ab_re - 1.0) * a_re + ab_im * a_im) / den
    z_im = (ab_im * a_re - (ab_re - 1.0) * a_im) / den
    Br, Bi = B_re.astype(F32), B_im.astype(F32)
    bb_re = z_re[..., None] * Br - z_im[..., None] * Bi
    bb_im = z_re[..., None] * Bi + z_im[..., None] * Br
    Cr, Ci = C_re.astype(F32), C_im.astype(F32)
    eye = jnp.eye(GP, dtype=BF16)

    ca_re = Cr[None] * pw_re[:L, :, None, :] - Ci[None] * pw_im[:L, :, None, :]
    ca_im = Cr[None] * pw_im[:L, :, None, :] + Ci[None] * pw_re[:L, :, None, :]
    kern = (jnp.einsum('lgcn,gnd->lgcd', ca_re, bb_re, precision=hp)
            - jnp.einsum('lgcn,gnd->lgcd', ca_im, bb_im, precision=hp))
    d_i = np.arange(L // 2)[:, None, None]
    la_i = np.arange(2)[None, :, None]
    lb_i = np.arange(2)[None, None, :]
    lag = 2 * d_i + lb_i - la_i
    tt = jnp.where((lag >= 0)[..., None, None, None], kern[np.clip(lag, 0, L - 1)], 0.0)
    tt = _bf(tt).reshape(L // 2, 2, 2, J, GP, W, W).transpose(0, 3, 1, 4, 6, 2, 5)
    tt = tt[:, :, :, :, :, :, None, :] * eye[None, None, None, :, None, None, :, None]
    tt = tt.reshape(L // 2, J, 2 * GP * W, 2 * GP * W)

    s_re = jnp.stack([ab_re[..., None] * bb_re - ab_im[..., None] * bb_im, bb_re])
    s_im = jnp.stack([ab_re[..., None] * bb_im + ab_im[..., None] * bb_re, bb_im])
    sb = jnp.stack([s_re, s_im])
    sb = _bf(sb).reshape(2, 2, J, GP, N, W).transpose(2, 1, 3, 5, 0, 4)
    sb = sb[:, :, :, :, :, None, :] * eye[None, None, :, None, None, :, None]
    sb = sb.reshape(J, 2 * GP * W, 2 * GP * N)

    o_re = jnp.stack([Cr, ca_re[1]])
    o_im = jnp.stack([Ci, ca_im[1]])
    cc = jnp.stack([o_re, -o_im])
    cc = _bf(cc).reshape(2, 2, J, GP, W, N).transpose(2, 0, 3, 5, 1, 4)
    cc = cc[:, :, :, :, :, None, :] * eye[None, None, :, None, None, :, None]
    cc = cc.reshape(J, 2 * GP * N, 2 * GP * W)

    sp_re, sp_im = apow(L * (2 ** np.arange(nsteps)))
    a12 = jnp.stack([pw_re[1], pw_im[1], pw_re[2], pw_im[2]]).reshape(4, G * N)
    return (tt, sb, cc, a12,
            sp_re.reshape(nsteps, G * N), sp_im.reshape(nsteps, G * N))


def _s5_kernel(u_ref, tt_ref, sb_ref, cc_ref, a12_ref, spre_ref, spim_ref, d_ref, wg_ref, bg_ref, o_ref,
               xp_ref, are_ref, aim_ref, bre_ref, bim_ref, ys_ref, *, K, pad, nsteps):
    L, J = S5_CHUNK, S5_TILES
    NP = L // 2
    TW = S5_WIDTH // J
    SW = S5_GROUPS * S5_STATE // J

    for l in range(L):
        for j in range(J):
            xp_ref[l // 2, j, :, (l % 2) * TW:(l % 2 + 1) * TW] = u_ref[0, :, l * S5_WIDTH + j * TW:l * S5_WIDTH + (j + 1) * TW]

    bufs = ((are_ref, aim_ref), (bre_ref, bim_ref))
    (fre, fim), (dre, dim) = bufs[nsteps % 2], bufs[(nsteps + 1) % 2]
    zeros = jnp.zeros((pad, SW), F32)

    def state_tile(j, carry):
        for r in (are_ref, aim_ref, bre_ref, bim_ref):
            r[j, 0:pad, :] = zeros
        a1r, a1i, a2r, a2i = a12_ref[j, 0:1, :], a12_ref[j, 1:2, :], a12_ref[j, 2:3, :], a12_ref[j, 3:4, :]
        acc_re = acc_im = None
        for a in range(NP):
            bu = _dot(xp_ref[a, j], sb_ref[j])
            if a == 0:
                acc_re, acc_im = bu[:, :SW], bu[:, SW:]
            else:
                acc_re, acc_im = (a2r * acc_re - a2i * acc_im + bu[:, :SW],
                                  a2r * acc_im + a2i * acc_re + bu[:, SW:])
        are_ref[j, pad:pad + K, :] = acc_re
        aim_ref[j, pad:pad + K, :] = acc_im
        for s in range(nsteps):
            d = 1 << s
            (sre, sim), (tre, tim) = bufs[s % 2], bufs[(s + 1) % 2]
            pr_, pi_ = spre_ref[j, s:s + 1, :], spim_ref[j, s:s + 1, :]
            cre, cim = sre[j, pad:pad + K, :], sim[j, pad:pad + K, :]
            hre, him = sre[j, pad - d:pad - d + K, :], sim[j, pad - d:pad - d + K, :]
            tre[j, pad:pad + K, :] = cre + pr_ * hre - pi_ * him
            tim[j, pad:pad + K, :] = cim + pr_ * him + pi_ * hre
        xre, xim = fre[j, pad - 1:pad - 1 + K, :], fim[j, pad - 1:pad - 1 + K, :]
        dre[j, pad:pad + K, :] = a1r * xre - a1i * xim
        dim[j, pad:pad + K, :] = a1r * xim + a1i * xre
        for b in range(NP):
            cre, cim = dre[j, pad:pad + K, :], dim[j, pad:pad + K, :]
            acc = _dot(jnp.concatenate([_bf(cre), _bf(cim)], axis=1), cc_ref[j])
            for a in range(b + 1):
                acc = acc + _dot(xp_ref[a, j], tt_ref[b - a, j])
            ys_ref[b, j] = acc
            if b + 1 < NP:
                dre[j, pad:pad + K, :] = a2r * cre - a2i * cim
                dim[j, pad:pad + K, :] = a2r * cim + a2i * cre
        return carry

    lax.fori_loop(0, J, state_tile, 0)

    def pair(b, carry):
        for lb in range(2):
            y = jnp.concatenate([ys_ref[b, j, :, lb * TW:(lb + 1) * TW] for j in range(J)], axis=1)
            u = jnp.concatenate([xp_ref[b, j, :, lb * TW:(lb + 1) * TW] for j in range(J)], axis=1)
            y = y + d_ref[...] * u.astype(F32)
            z = 0.5 * y * (1.0 + jnp.tanh(math.sqrt(2.0 / math.pi) * (y + 0.044715 * (y * y * y))))
            o_ref[0, b, :, lb * S5_WIDTH:(lb + 1) * S5_WIDTH] = _bf(
                z * _sigmoid(_dot(_bf(z), wg_ref[...]) + bg_ref[...]))
        return carry

    lax.fori_loop(0, NP, pair, 0)


def _s5(ub, tabs, D, w_glu, b_glu):
    B, S, _ = ub.shape
    L, J = S5_CHUNK, S5_TILES
    NP = L // 2
    K = S // L
    tt, sb, cc, a12, sp_re, sp_im = tabs
    nsteps = sp_re.shape[0]
    pad = max(8, K // 2)
    SW = S5_GROUPS * S5_STATE // J
    tile_major = lambda t: t.reshape(t.shape[0], J, SW).transpose(1, 0, 2)
    small = [tt, sb, cc, tile_major(a12), tile_major(sp_re), tile_major(sp_im),
             D.reshape(1, -1), _bf(w_glu), b_glu.reshape(1, -1)]
    out = pl.pallas_call(
        functools.partial(_s5_kernel, K=K, pad=pad, nsteps=nsteps),
        name="s5",
        grid=(B,),
        in_specs=[pl.BlockSpec((1, K, L * S5_WIDTH), lambda b: (b, 0, 0))]
        + [_full(a.shape, single=True) for a in small],
        out_specs=pl.BlockSpec((1, NP, K, 2 * S5_WIDTH), lambda b: (b, 0, 0, 0)),
        out_shape=jax.ShapeDtypeStruct((B, NP, K, 2 * S5_WIDTH), BF16),
        scratch_shapes=[pltpu.VMEM((NP, J, K, 2 * S5_WIDTH // J), BF16)]
        + [pltpu.VMEM((J, pad + K, SW), F32)] * 4
        + [pltpu.VMEM((NP, J, K, 2 * S5_WIDTH // J), F32)],
        compiler_params=_cparams("parallel"),
    )(ub.reshape(B, K, L * S5_WIDTH), *small)
    return out.transpose(0, 2, 1, 3).reshape(B * S, S5_WIDTH)


def _rwkv_tables(C, nsub):
    GH, HD, GW = RW_GROUP_HEADS, RW_HEAD, RW_GROUP_W
    t = np.arange(C)[:, None]
    s = np.arange(GW)[None, :] % C
    lane = np.arange(GW)
    lanemask = np.zeros((8, GW), np.float32)
    for h in range(GH):
        lanemask[h] = (lane // HD) == h
    bd = ((lane[:, None] // HD) == (lane[None, :] // HD)).astype(np.float32)
    c = np.arange(RW_WIDTH)
    ones = ((c[:, None] // HD) == (c[None, :] // HD)).astype(np.float32)
    r = np.arange(nsub * C)
    tri = ((r[:, None] >= r[None, :]) & ((r[:, None] // C) == (r[None, :] // C))).astype(np.float32)
    return (jnp.asarray((s < t), F32), jnp.asarray((s <= t), F32), jnp.asarray((s == t), F32),
            jnp.asarray(bd), jnp.asarray(lanemask, BF16), jnp.asarray(ones, BF16), jnp.asarray(tri, BF16))


def _rwkv_kernel(*refs, C, nsub, has_vres):
    if has_vres:
        (p_ref, vf_ref, mu_ref, w0_ref, wup_ref, a0_ref, aup_ref, gup_ref, kk_ref, ka_ref, rk_ref,
         lnw_ref, lnb_ref, vup_ref, vb_ref,
         strict_ref, incl_ref, eye_ref, bd_ref, lm_ref, ones_ref, tri_ref, o_ref, st_ref, prev_ref) = refs
    else:
        (p_ref, mu_ref, w0_ref, wup_ref, a0_ref, aup_ref, gup_ref, kk_ref, ka_ref, rk_ref,
         lnw_ref, lnb_ref,
         strict_ref, incl_ref, eye_ref, bd_ref, lm_ref, ones_ref, tri_ref, o_ref, vout_ref, st_ref, prev_ref) = refs

    @pl.when(pl.program_id(1) == 0)
    def _():
        st_ref[...] = jnp.zeros_like(st_ref)
        prev_ref[...] = jnp.zeros_like(prev_ref)

    W = RW_WIDTH
    R = nsub * C
    x = p_ref[0]
    rolled = pltpu.roll(x, 1, axis=0)
    row = lax.broadcasted_iota(jnp.int32, x.shape, 0)
    shifted = jnp.where(row == 0, prev_ref[0:1, :], rolled)
    prev_ref[0:1, :] = x[R - 1:R, :]
    m = x + (shifted - x) * mu_ref[...]
    r, k, v = m[:, 0:W], m[:, W:2 * W], m[:, 2 * W:3 * W]
    wd, ad, gd = m[:, 3 * W:3 * W + 128], m[:, 3 * W + 128:3 * W + 256], m[:, 3 * W + 256:3 * W + 384]
    zw = -(w0_ref[...] + _dot(_bf(jnp.tanh(wd)), wup_ref[...]))
    softplus = jnp.maximum(zw, 0.0) + jnp.log(1.0 + jnp.exp(-jnp.abs(zw)))
    logw = -jnp.exp(-softplus - 0.5)
    lr = _sigmoid(a0_ref[...] + _dot(_bf(ad), aup_ref[...]))
    gate = _dot(_bf(_sigmoid(gd)), gup_ref[...])
    if has_vres:
        vl = x[:, 3 * W + 384:3 * W + 512]
        v = v + (vf_ref[0] - v) * _sigmoid(vb_ref[...] + _dot(_bf(vl), vup_ref[...]))
    else:
        vout_ref[0] = v
    ones = ones_ref[...]
    kn = k * kk_ref[...]
    kn = kn * lax.rsqrt(_dot_exact_rhs(kn * kn, ones) + 1e-12)
    k2 = k * (1.0 + (lr - 1.0) * ka_ref[...])
    cum = _dot_exact_lhs(tri_ref[...], logw)
    p_inv = jnp.exp(-cum)
    a_all = -kn * jnp.exp(cum - logw)
    r_all = r * jnp.exp(cum)
    b_all = kn * lr * p_inv
    k_all = k2 * p_inv
    knlr = kn * lr

    strict, incl, eye, bd = strict_ref[...], incl_ref[...], eye_ref[...], bd_ref[...]
    GW = RW_GROUP_W
    NG = RW_HEADS // RW_GROUP_HEADS
    HC = RW_GROUP_HEADS * C

    def stack(z):
        zb = _bf(z)
        return jnp.concatenate([zb * lm_ref[h:h + 1, :] for h in range(RW_GROUP_HEADS)], axis=0)

    chains = [(sc, gi) for sc in range(nsub) for gi in range(NG)]
    rsl = lambda sc: slice(sc * C, (sc + 1) * C)
    lsl = lambda gi: slice(gi * GW, (gi + 1) * GW)
    ar, vs, n_w, aak, arbk = {}, {}, {}, {}, {}
    for c in chains:
        rs, sl = rsl(c[0]), lsl(c[1])
        ar[c] = _bf(jnp.concatenate([a_all[rs, sl], r_all[rs, sl]], axis=0))
        vs[c] = stack(v[rs, sl])
        ybk = jnp.concatenate([stack(b_all[rs, sl]), stack(k_all[rs, sl])], axis=0)
        p1 = _dot_nt(ar[c], ybk)
        n_w[c] = p1[0:C, 0:HC] * strict
        aak[c] = _bf(p1[0:C, HC:2 * HC] * strict)
        arbk[c] = _bf(jnp.concatenate([p1[C:2 * C, 0:HC] * incl, p1[C:2 * C, HC:2 * HC] * incl], axis=1))
    tinv = {c: eye + n_w[c] for c in chains}
    pw = dict(n_w)
    for _ in range(int(math.log2(C)) - 1):
        pw = {c: _dot(_bf(pw[c]), stack(pw[c])) for c in chains}
        tinv = {c: tinv[c] + _dot(_bf(tinv[c]), stack(pw[c])) for c in chains}
    zv = {c: _dot(aak[c], vs[c]) for c in chains}
    tinv = {c: _bf(tinv[c]) for c in chains}

    states = [st_ref[gi] for gi in range(NG)]
    out_rows = []
    for sc in range(nsub):
        rs = rsl(sc)
        cum_last = cum[(sc + 1) * C - 1:(sc + 1) * C, :]
        p_end = jnp.exp(cum_last - cum[rs])
        p_last = jnp.exp(cum_last)
        bke = {gi: _bf(jnp.concatenate([knlr[rs, lsl(gi)] * p_end[:, lsl(gi)], k2[rs, lsl(gi)] * p_end[:, lsl(gi)]], axis=0))
               for gi in range(NG)}
        zr = {gi: _dot_nt(ar[(sc, gi)], _bf(states[gi])) for gi in range(NG)}
        u = {gi: _dot(tinv[(sc, gi)], stack(zr[gi][0:C] + zv[(sc, gi)])) for gi in range(NG)}
        outs = [zr[gi][C:2 * C] + _dot(arbk[(sc, gi)], jnp.concatenate([stack(u[gi]), vs[(sc, gi)]], axis=0))
                for gi in range(NG)]
        for gi in range(NG):
            upd = _dot_tn(_bf(jnp.concatenate([u[gi], v[rs, lsl(gi)]], axis=0)), bke[gi])
            states[gi] = states[gi] * p_last[:, lsl(gi)] + upd * bd
        out_rows.append(jnp.concatenate(outs, axis=1))
    for gi in range(NG):
        st_ref[gi] = states[gi]
    y = jnp.concatenate(out_rows, axis=0)
    inv_n = 1.0 / RW_HEAD
    mean = _dot_exact_rhs(y, ones) * inv_n
    yc = y - mean
    var = _dot_exact_rhs(yc * yc, ones) * inv_n
    yn = yc * lax.rsqrt(var + RW_GN_EPS) * lnw_ref[...] + lnb_ref[...]
    bonus = _dot_exact_rhs(r * k2 * rk_ref[...], ones) * v
    o_ref[0] = _bf((yn + bonus) * gate)


def _rwkv(p3, v_first, prm, *, C=RW_CHUNK, nsub=RW_SUBCHUNKS):
    B, S, _ = p3.shape
    C = min(C, S)
    R = nsub * C
    assert S % R == 0
    has_vres = v_first is not None
    tabs = _rwkv_tables(C, nsub)
    W = RW_WIDTH
    NG = RW_HEADS // RW_GROUP_HEADS
    row = lambda a: a.reshape(1, -1)
    args = [p3]
    specs = [pl.BlockSpec((1, R, 2048), lambda b, i: (b, i, P_RW // 2048))]
    if has_vres:
        args.append(v_first)
        specs.append(pl.BlockSpec((1, R, W), lambda b, i: (b, i, 0)))
    small = [row(prm["mu"]), row(prm["w0"]), prm["w_up"], row(prm["a0"]), prm["a_up"], prm["g_up"],
             row(prm["k_k"]), row(prm["k_a"]), row(prm["r_k"]), row(prm["ln_w"]), row(prm["ln_b"])]
    if has_vres:
        small += [prm["v_up"], row(prm["v_bias"])]
    small += list(tabs)
    args += small
    specs += [_full(a.shape) for a in small]
    o_spec = pl.BlockSpec((1, R, W), lambda b, i: (b, i, 0))
    if has_vres:
        out_specs, out_shape = o_spec, jax.ShapeDtypeStruct((B, S, W), BF16)
    else:
        out_specs = [o_spec, o_spec]
        out_shape = [jax.ShapeDtypeStruct((B, S, W), BF16), jax.ShapeDtypeStruct((B, S, W), F32)]
    res = pl.pallas_call(
        functools.partial(_rwkv_kernel, C=C, nsub=nsub, has_vres=has_vres),
        name="rwkv7",
        grid=(B, S // R),
        in_specs=specs, out_specs=out_specs, out_shape=out_shape,
        scratch_shapes=[pltpu.VMEM((NG, RW_GROUP_W, RW_GROUP_W), F32), pltpu.VMEM((8, 2048), F32)],
        compiler_params=_cparams("parallel", "arbitrary"),
    )(*args)
    return (res, v_first) if has_vres else (res[0], res[1])


def _merge_kernel(x_ref, o0_ref, o1_ref, o2_ref, o3_ref, g_ref, wg_ref, wb_ref, wo_ref, y_ref):
    x = x_ref[...]
    D = x.shape[1]
    hn = _bf(_rms(x, g_ref[...]))
    acc = None
    for m, o_ref in enumerate((o0_ref, o1_ref, o2_ref, o3_ref)):
        gate = _sigmoid(_dot(hn, wg_ref[:, m * D:(m + 1) * D]))
        t = gate * _dot(o_ref[...], wb_ref[m])
        acc = t if acc is None else acc + t
    y_ref[...] = x + _dot(_bf(acc), wo_ref[...])


def _merge(x, g, outs, wg, wb, wo, *, tm=256):
    T, D = x.shape
    tm = min(tm, T)
    bw = outs[0].shape[1]
    return pl.pallas_call(
        _merge_kernel,
        name="gated_merge",
        grid=(T // tm,),
        in_specs=[pl.BlockSpec((tm, D), lambda i: (i, 0))]
        + [pl.BlockSpec((tm, bw), lambda i: (i, 0))] * 4
        + [_full((1, D)), _full(wg.shape), _full(wb.shape), _full(wo.shape)],
        out_specs=pl.BlockSpec((tm, D), lambda i: (i, 0)),
        out_shape=jax.ShapeDtypeStruct((T, D), F32),
        compiler_params=_cparams("parallel"),
    )(x, *outs, g.reshape(1, D), wg, wb, wo)


def _xattn_kernel(x_ref, kv_ref, g_ref, wq_ref, wo_ref, y_ref):
    x = x_ref[0]
    D = x.shape[1]
    hd = D // X_HEADS
    q = _bf(_dot(_bf(_rms(x, g_ref[...])), wq_ref[...]) * (hd ** -0.5))
    outs = []
    for h in range(X_HEADS):
        kh = kv_ref[0, :, h * hd:(h + 1) * hd]
        vh = kv_ref[0, :, D + h * hd:D + (h + 1) * hd]
        s = _dot_nt(q[:, h * hd:(h + 1) * hd], kh)
        e = jnp.exp(s - jnp.max(s, axis=-1, keepdims=True))
        pr = e / jnp.sum(e, axis=-1, keepdims=True)
        outs.append(_bf(_dot(_bf(pr), vh)))
    y_ref[0] = x + _dot(jnp.concatenate(outs, axis=1), wo_ref[...])


def _xattn(x3, kv3, g, wq, wo, *, tm=512):
    B, S, D = x3.shape
    M = kv3.shape[1]
    tm = min(tm, S)
    return pl.pallas_call(
        _xattn_kernel,
        name="cross_attn",
        grid=(B, S // tm),
        in_specs=[pl.BlockSpec((1, tm, D), lambda b, i: (b, i, 0)),
                  pl.BlockSpec((1, M, 2 * D), lambda b, i: (b, 0, 0)),
                  _full((1, D)), _full(wq.shape), _full(wo.shape)],
        out_specs=pl.BlockSpec((1, tm, D), lambda b, i: (b, i, 0)),
        out_shape=jax.ShapeDtypeStruct((B, S, D), F32),
        compiler_params=_cparams("parallel", "parallel"),
    )(x3, kv3, g.reshape(1, D), wq, wo)


HALO = 16


def _ffn_kernel(x_ref, halo_ref, g_ref, wg_ref, wu_ref, cw_ref, cb_ref, wd_ref, gf_ref, y_ref,
                hn_ref, gbuf_ref, acc_ref, *, tm, final_norm):
    i = pl.program_id(1)
    f = pl.program_id(2)

    @pl.when(f == 0)
    def _():
        hn_ref[0:HALO, :] = _bf(_rms(halo_ref[0], g_ref[...]))
        hn_ref[HALO:HALO + tm, :] = _bf(_rms(x_ref[0], g_ref[...]))
        acc_ref[...] = jnp.zeros_like(acc_ref)

    hn = hn_ref[...]
    gpre = _dot(hn, wg_ref[...])
    keep = (i > 0).astype(F32)
    gbuf_ref[0:HALO, :] = gpre[0:HALO] * keep
    gbuf_ref[HALO:HALO + tm, :] = gpre[HALO:HALO + tm]
    conv = (cw_ref[0:1, :] * gbuf_ref[HALO - 2:HALO - 2 + tm, :]
            + cw_ref[1:2, :] * gbuf_ref[HALO - 1:HALO - 1 + tm, :]
            + cw_ref[2:3, :] * gbuf_ref[HALO:HALO + tm, :] + cb_ref[...])
    up = _dot(hn[HALO:HALO + tm], wu_ref[...])
    act = conv * _sigmoid(conv) * up
    acc_ref[...] += _dot(_bf(act), wd_ref[...])

    @pl.when(f == pl.num_programs(2) - 1)
    def _():
        y = x_ref[0] + acc_ref[...]
        if final_norm:
            y = _rms(y, gf_ref[...])
        y_ref[0] = y


def _ffn(x3, g, wg, wu, cw, cb, wd, gf, *, final_norm, tm=512, tf=1408):
    B, S, D = x3.shape
    F = wg.shape[1]
    tm = min(tm, S)
    hb = tm // HALO
    return pl.pallas_call(
        functools.partial(_ffn_kernel, tm=tm, final_norm=final_norm),
        name="conv_ffn",
        grid=(B, S // tm, F // tf),
        in_specs=[pl.BlockSpec((1, tm, D), lambda b, i, f: (b, i, 0)),
                  pl.BlockSpec((1, HALO, D), lambda b, i, f: (b, jnp.maximum(i * hb - 1, 0), 0)),
                  pl.BlockSpec((1, D), lambda b, i, f: (0, 0)),
                  pl.BlockSpec((D, tf), lambda b, i, f: (0, f)),
                  pl.BlockSpec((D, tf), lambda b, i, f: (0, f)),
                  pl.BlockSpec((3, tf), lambda b, i, f: (0, f)),
                  pl.BlockSpec((1, tf), lambda b, i, f: (0, f)),
                  pl.BlockSpec((tf, D), lambda b, i, f: (f, 0)),
                  pl.BlockSpec((1, D), lambda b, i, f: (0, 0))],
        out_specs=pl.BlockSpec((1, tm, D), lambda b, i, f: (b, i, 0)),
        out_shape=jax.ShapeDtypeStruct((B, S, D), F32),
        scratch_shapes=[pltpu.VMEM((HALO + tm, D), BF16), pltpu.VMEM((HALO + tm, tf), F32),
                        pltpu.VMEM((tm, D), F32)],
        compiler_params=_cparams("parallel", "parallel", "arbitrary"),
    )(x3, x3, g.reshape(1, D), wg, wu, cw, cb.reshape(1, F), wd, gf.reshape(1, D))


def _pad_cols(w, n):
    return jnp.pad(w, ((0, 0), (0, n - w.shape[1])))


def _pad_rows(w, n):
    return jnp.pad(w, ((0, n - w.shape[0]), (0, 0)))


def _swap_halves(w):
    h = w.shape[-1] // 2
    return jnp.concatenate([w[..., h:], w[..., :h]], axis=-1)


def _pack_w1(w_in, v_down):
    D = w_in.shape[0]
    w_in = _bf(w_in)
    if v_down is not None:
        v_down = _bf(v_down)
    o = 0
    cq = w_in[:, o:o + 256]; o += 256
    ckv = w_in[:, o:o + 128]; o += 128
    kr = w_in[:, o:o + 64]; o += 64
    hg = w_in[:, o:o + 2048]; o += 2048
    su = w_in[:, o:o + 512]; o += 512
    rw = w_in[:, o:o + 1792]; o += 1792
    gates = w_in[:, o:o + 4096]
    rw_p = jnp.concatenate([rw[:, 0:1536], _pad_cols(rw[:, 1536:1600], 128), _pad_cols(rw[:, 1600:1664], 128),
                            rw[:, 1664:1792],
                            _pad_cols(v_down, 128) if v_down is not None else jnp.zeros((D, 128), w_in.dtype)],
                           axis=1)
    mla = jnp.concatenate([cq, ckv, kr, _swap_halves(kr)], axis=1)
    w1 = jnp.concatenate([hg, rw_p, mla, su], axis=1)
    assert w1.shape[1] == P_WIDTH
    return w1, gates


def _pack_mla(w_uq, w_ukv):
    H = MLA_HEADS
    wq = w_uq.reshape(MLA_Q_RANK, H, MLA_QK)
    nope, pe = wq[..., :MLA_NOPE], wq[..., MLA_NOPE:]
    z = jnp.zeros((MLA_Q_RANK, H, 64), w_uq.dtype)
    wq_p = jnp.concatenate([nope, pe, z, _swap_halves(pe), z], axis=-1).reshape(MLA_Q_RANK, H * 384)
    return _bf(wq_p), _bf(w_ukv)


def kernel(x, mem, positions, norm_mix, w_in, mla_q_norm, mla_w_uq, mla_kv_norm, mla_w_ukv, hgrn_lb_logits, hgrn_o_norm, s5_A_re, s5_A_im, s5_log_step, s5_B_re, s5_B_im, s5_C_re, s5_C_im, s5_D, s5_w_glu, s5_b_glu, rwkv_mu, rwkv_w0, rwkv_w_up, rwkv_a0, rwkv_a_up, rwkv_g_up, rwkv_k_k, rwkv_k_a, rwkv_r_k, rwkv_ln_w, rwkv_ln_b, rwkv_vres_down, rwkv_vres_up, rwkv_vres_bias, w_branch_mla, w_branch_hgrn, w_branch_s5, w_branch_rwkv, w_out, norm_xq, norm_xm, xattn_w_q, xattn_w_kv, xattn_w_o, norm_ffn, ffn_w_gate_up, ffn_conv_w, ffn_conv_b, ffn_w_down, norm_final):
    B, S, D = x.shape
    T = B * S
    depth = norm_mix.shape[0]
    M = mem.shape[1]

    lb_p = jax.nn.softmax(hgrn_lb_logits.astype(F32), axis=0)
    lb_c = jnp.cumsum(lb_p, axis=0)
    lower_bounds = lb_c - lb_c[0:1]

    half = MLA_ROPE // 2
    inv_freq = ROPE_THETA ** (-np.arange(half, dtype=np.float32) / half)
    frq = jnp.asarray(np.concatenate([inv_freq, inv_freq])[None, :], F32)
    sgn = jnp.asarray(np.concatenate([-np.ones(half), np.ones(half)])[None, :], F32)
    pos3 = positions.astype(F32).reshape(B, S, 1)

    nsteps = int(math.log2(S // S5_CHUNK))
    mem2 = mem.reshape(B * M, D)
    x2 = x.reshape(T, D)
    v_first = None
    for l in range(depth):
        w1, wgates = _pack_w1(w_in[l], rwkv_vres_down[l - 1] if l > 0 else None)
        p, ub = _in_proj(x2, norm_mix[l], w1)
        p3 = p.reshape(B, S, P_S5)

        wq_p, wkv_p = _pack_mla(mla_w_uq[l], mla_w_ukv[l])
        q, k, v = _mla_prep(p3, pos3, frq, sgn, mla_q_norm[l].reshape(1, -1), wq_p,
                            mla_kv_norm[l].reshape(1, -1), wkv_p)
        o_mla = _mla_attn(q, k, v).reshape(T, -1)

        o_hg = _hgrn(p3, lower_bounds[l], hgrn_o_norm[l]).reshape(T, -1)

        s5_tabs = _s5_tables(s5_A_re[l], s5_A_im[l], s5_log_step[l], s5_B_re[l], s5_B_im[l],
                             s5_C_re[l], s5_C_im[l], nsteps)
        o_s5 = _s5(ub.reshape(B, S, S5_WIDTH), s5_tabs, s5_D[l], s5_w_glu[l], s5_b_glu[l])

        mu = rwkv_mu[l]
        mu_p = jnp.concatenate([mu[0:1536], jnp.pad(mu[1536:1600], (0, 64)), jnp.pad(mu[1600:1664], (0, 64)),
                                mu[1664:1792], jnp.zeros((128,), mu.dtype)])
        prm = dict(mu=mu_p, w0=rwkv_w0[l], w_up=_bf(_pad_rows(rwkv_w_up[l], 128)), a0=rwkv_a0[l],
                   a_up=_bf(_pad_rows(rwkv_a_up[l], 128)), g_up=_bf(rwkv_g_up[l]), k_k=rwkv_k_k[l],
                   k_a=rwkv_k_a[l], r_k=rwkv_r_k[l], ln_w=rwkv_ln_w[l], ln_b=rwkv_ln_b[l])
        if l > 0:
            prm["v_up"] = _bf(_pad_rows(rwkv_vres_up[l - 1], 128))
            prm["v_bias"] = rwkv_vres_bias[l - 1]
        o_rw, v_first = _rwkv(p3, v_first, prm)
        o_rw = o_rw.reshape(T, -1)

        wb = _bf(jnp.stack([w_branch_mla[l], w_branch_hgrn[l], w_branch_s5[l], w_branch_rwkv[l]]))
        x2 = _merge(x2, norm_mix[l], (o_mla, o_hg, o_s5, o_rw), wgates, wb, _bf(w_out[l]))

        kv = _norm_proj(mem2, norm_xm[l], _bf(xattn_w_kv[l]), out_dtype=BF16, name="xattn_kv_proj")
        x3 = _xattn(x2.reshape(B, S, D), kv.reshape(B, M, 2 * D), norm_xq[l], _bf(xattn_w_q[l]),
                    _bf(xattn_w_o[l]))

        wgu = ffn_w_gate_up[l]
        x3 = _ffn(x3, norm_ffn[l], _bf(wgu[:, :D_FF]), _bf(wgu[:, D_FF:]), ffn_conv_w[l], ffn_conv_b[l],
                  _bf(ffn_w_down[l]), norm_final, final_norm=(l == depth - 1))
        x2 = x3.reshape(T, D)
    return x2.reshape(B, S, D)
```
